```python
import jax, jax.numpy as jnp
from jax import lax
import numpy as np

D_MODEL = 1024
BATCH = 4
SEQ = 4096
DEPTH = 4
DEC_BATCH = 32
DEC_SEQ = 1
PAST_LEN = 8192
PAGE_SIZE = 128

N_MIXERS = 2
N_POOL_LAYERS = (DEPTH + 1) // 2
N_FOX_LAYERS = DEPTH // 2
POOL_WINDOWS = (2, 4, 8, 16)
N_POOL_GROUPS = len(POOL_WINDOWS)
POOL_GROUP = D_MODEL // N_POOL_GROUPS
POOL_BUF = max(POOL_WINDOWS) - 1
N_HEADS = 16
HEAD_DIM = D_MODEL // N_HEADS
D_FF = 4 * D_MODEL
Q_BLOCK = 128
RMS_EPS = 1e-6

kernel_name = "hybrid_pool_fox_decoder_step"


def rmsnorm(x, g):
    xf = x.astype(jnp.float32)
    r = lax.rsqrt(jnp.mean(xf * xf, axis=-1, keepdims=True) + RMS_EPS)
    return (xf * r).astype(x.dtype) * g


def pool_mix(h, buf, pos0, w_pool, scale):
    B, T, _ = h.shape
    h_ext = jnp.concatenate([buf, h], axis=1)
    c = jnp.cumsum(h_ext.astype(jnp.float32), axis=1)
    c = jnp.pad(c, ((0, 0), (1, 0), (0, 0)))
    pos = pos0 + jnp.arange(T)
    outs = []
    for g, w in enumerate(POOL_WINDOWS):
        cg = c[..., g * POOL_GROUP:(g + 1) * POOL_GROUP]
        s = cg[:, POOL_BUF + 1:POOL_BUF + 1 + T] - cg[:, POOL_BUF + 1 - w:POOL_BUF + 1 - w + T]
        cnt = jnp.minimum(w, pos + 1).astype(jnp.float32)
        outs.append(s / cnt[None, :, None])
    pooled = jnp.concatenate(outs, axis=-1).astype(h.dtype) - h
    y = jnp.einsum('btgc,gcd->btgd', pooled.reshape(B, T, N_POOL_GROUPS, POOL_GROUP), w_pool)
    y = y.reshape(B, T, D_MODEL) * scale
    return y, h_ext[:, -POOL_BUF:]


def fox_project(h, w_qkv, w_f, b_f):
    B, T, _ = h.shape
    qkv = h @ w_qkv
    q, k, v = jnp.split(qkv, 3, axis=-1)
    q = q.reshape(B, T, N_HEADS, HEAD_DIM)
    k = k.reshape(B, T, N_HEADS, HEAD_DIM)
    v = v.reshape(B, T, N_HEADS, HEAD_DIM)
    logf = jax.nn.log_sigmoid((h @ w_f + b_f).astype(jnp.float32))
    return q, k, v, logf


def fox_attend(qb, Fq, qpos, k, v, Fk, kpos):
    s = jnp.einsum('bqhd,bkhd->bhqk', qb, k).astype(jnp.float32) * (HEAD_DIM ** -0.5)
    bias = jnp.transpose(Fq, (0, 2, 1))[..., :, None] - jnp.transpose(Fk, (0, 2, 1))[..., None, :]
    s = jnp.where(kpos[None, :] <= qpos[:, None], s + bias, -jnp.inf)
    p = jax.nn.softmax(s, axis=-1)
    return jnp.einsum('bhqk,bkhd->bqhd', p.astype(v.dtype), v)


def fox_prompt(h, w_qkv, w_f, b_f, w_o):
    B, T, _ = h.shape
    q, k, v, logf = fox_project(h, w_qkv, w_f, b_f)
    F = jnp.cumsum(logf, axis=1)
    kpos = jnp.arange(T)

    def block(i):
        start = i * Q_BLOCK
        qb = lax.dynamic_slice_in_dim(q, start, Q_BLOCK, axis=1)
        Fq = lax.dynamic_slice_in_dim(F, start, Q_BLOCK, axis=1)
        qpos = start + jnp.arange(Q_BLOCK)
        return fox_attend(qb, Fq, qpos, k, v, F, kpos)

    o = lax.map(block, jnp.arange(T // Q_BLOCK))
    o = jnp.transpose(o, (1, 0, 2, 3, 4)).reshape(B, T, D_MODEL)
    return o @ w_o, k, v, logf


def fox_sample(h, ck, cv, clf, page_table, w_qkv, w_f, b_f, w_o):
    B, T, _ = h.shape
    past = page_table.shape[1] * PAGE_SIZE
    q, k, v, logf = fox_project(h, w_qkv, w_f, b_f)
    k_past = ck[page_table].reshape(B, past, N_HEADS, HEAD_DIM)
    v_past = cv[page_table].reshape(B, past, N_HEADS, HEAD_DIM)
    lf_past = clf[page_table].reshape(B, past, N_HEADS).astype(jnp.float32)
    k_all = jnp.concatenate([k_past, k], axis=1)
    v_all = jnp.concatenate([v_past, v], axis=1)
    F = jnp.cumsum(jnp.concatenate([lf_past, logf], axis=1), axis=1)
    kpos = jnp.arange(past + T)
    qpos = past + jnp.arange(T)
    o = fox_attend(q, F[:, past:], qpos, k_all, v_all, F, kpos).reshape(B, T, D_MODEL)
    return o @ w_o, k, v, logf


def sq_relu_mlp(h, w_up, w_down):
    u = jax.nn.relu(h @ w_up)
    return (u * u) @ w_down


def setup_inputs(seed: int = 0) -> dict:
    key = jax.random.key(seed)
    ks = jax.random.split(key, 20)
    n_pages = PAST_LEN // PAGE_SIZE
    n_used = DEC_BATCH * n_pages
    n_phys = (n_used * 5) // 4
    f32 = jnp.float32
    x_prompt = jax.random.normal(ks[0], (BATCH, SEQ, D_MODEL), f32)
    x_sample = jax.random.normal(ks[1], (DEC_BATCH, DEC_SEQ, D_MODEL), f32)
    cache_k = jax.random.normal(ks[2], (N_FOX_LAYERS, n_phys, PAGE_SIZE, N_HEADS, HEAD_DIM), f32)
    cache_v = jax.random.normal(ks[3], (N_FOX_LAYERS, n_phys, PAGE_SIZE, N_HEADS, HEAD_DIM), f32)
    cache_logf = jax.nn.log_sigmoid(3.0 + jax.random.normal(ks[4], (N_FOX_LAYERS, n_phys, PAGE_SIZE, N_HEADS), f32))
    state_pool = jax.random.normal(ks[5], (N_POOL_LAYERS, DEC_BATCH, POOL_BUF, D_MODEL), f32)
    page_table = jax.random.permutation(ks[6], n_phys)[:n_used].reshape(DEC_BATCH, n_pages).astype(jnp.int32)
    norm_mix = 1.0 + 0.1 * jax.random.normal(ks[7], (DEPTH, D_MODEL), f32)
    norm_mlp = 1.0 + 0.1 * jax.random.normal(ks[8], (DEPTH, D_MODEL), f32)
    norm_final = 1.0 + 0.1 * jax.random.normal(ks[9], (D_MODEL,), f32)
    pool_w = jax.random.normal(ks[10], (N_POOL_LAYERS, N_POOL_GROUPS, POOL_GROUP, POOL_GROUP), f32) * POOL_GROUP ** -0.5
    pool_scale = 1.0 + 0.1 * jax.random.normal(ks[11], (N_POOL_LAYERS, D_MODEL), f32)
    w_qkv = jax.random.normal(ks[12], (N_FOX_LAYERS, D_MODEL, 3 * D_MODEL), f32) * D_MODEL ** -0.5
    w_f = jax.random.normal(ks[13], (N_FOX_LAYERS, D_MODEL, N_HEADS), f32) * D_MODEL ** -0.5
    b_f = jnp.linspace(1.0, 6.0, N_HEADS, dtype=f32)[None, :] + 0.1 * jax.random.normal(ks[14], (N_FOX_LAYERS, N_HEADS), f32)
    w_o = jax.random.normal(ks[15], (N_FOX_LAYERS, D_MODEL, D_MODEL), f32) * D_MODEL ** -0.5
    w_up = jax.random.normal(ks[16], (DEPTH, D_MODEL, D_FF), f32) * D_MODEL ** -0.5
    w_down = jax.random.normal(ks[17], (DEPTH, D_FF, D_MODEL), f32) * D_FF ** -0.5
    return {"x_prompt": x_prompt, "x_sample": x_sample, "cache_k": cache_k, "cache_v": cache_v,
            "cache_logf": cache_logf, "state_pool": state_pool, "page_table": page_table,
            "norm_mix": norm_mix, "norm_mlp": norm_mlp, "norm_final": norm_final,
            "pool_w": pool_w, "pool_scale": pool_scale, "w_qkv": w_qkv, "w_f": w_f, "b_f": b_f,
            "w_o": w_o, "w_up": w_up, "w_down": w_down}


def reference(x_prompt, x_sample, cache_k, cache_v, cache_logf, state_pool, page_table,
              norm_mix, norm_mlp, norm_final, pool_w, pool_scale, w_qkv, w_f, b_f, w_o,
              w_up, w_down):
    xp, xs = x_prompt, x_sample
    past = page_table.shape[1] * PAGE_SIZE
    kp, vp, lp, pp = [], [], [], []
    ks_, vs_, ls_, ps_ = [], [], [], []
    for i in range(DEPTH):
        j = i // N_MIXERS
        hp = rmsnorm(xp, norm_mix[i])
        hs = rmsnorm(xs, norm_mix[i])
        if i % N_MIXERS == 0:
            buf0 = jnp.zeros((hp.shape[0], POOL_BUF, D_MODEL), hp.dtype)
            yp, bp = pool_mix(hp, buf0, 0, pool_w[j], pool_scale[j])
            ys, bs = pool_mix(hs, state_pool[j], past, pool_w[j], pool_scale[j])
            pp.append(bp)
            ps_.append(bs)
        else:
            yp, k1, v1, l1 = fox_prompt(hp, w_qkv[j], w_f[j], b_f[j], w_o[j])
            ys, k2, v2, l2 = fox_sample(hs, cache_k[j], cache_v[j], cache_logf[j], page_table,
                                        w_qkv[j], w_f[j], b_f[j], w_o[j])
            kp.append(k1); vp.append(v1); lp.append(l1)
            ks_.append(k2); vs_.append(v2); ls_.append(l2)
        xp = xp + yp
        xs = xs + ys
        xp = xp + sq_relu_mlp(rmsnorm(xp, norm_mlp[i]), w_up[i], w_down[i])
        xs = xs + sq_relu_mlp(rmsnorm(xs, norm_mlp[i]), w_up[i], w_down[i])
    y_prompt = rmsnorm(xp, norm_final)
    y_sample = rmsnorm(xs, norm_final)
    return (y_prompt, y_sample,
            jnp.stack(kp), jnp.stack(vp), jnp.stack(lp), jnp.stack(pp),
            jnp.stack(ks_), jnp.stack(vs_), jnp.stack(ls_), jnp.stack(ps_))
```

```python
import functools

import numpy as np
import jax
import jax.numpy as jnp
from jax import lax
from jax.experimental import pallas as pl
from jax.experimental.pallas import tpu as pltpu

F32 = jnp.float32
BF16 = jnp.bfloat16

D_MODEL = 1024
N_HEADS = 16
HEAD_DIM = 64
D_FF = 4 * D_MODEL
POOL_WINDOWS = (2, 4, 8, 16)
POOL_GROUP = D_MODEL // len(POOL_WINDOWS)
POOL_BUF = max(POOL_WINDOWS) - 1
PAGE_SIZE = 128
RMS_EPS = 1e-6

LANES = 128
HALO = 16
VMEM_LIMIT = 56 * 1024 * 1024

LOG2E = 1.4426950408889634
Q_SCALE = HEAD_DIM ** -0.5 * LOG2E

MLP_ROWS = 512
MLP_FF_CHUNK = 1024
POOL_ROWS = 512
PROJ_ROWS = 256
ATTN_Q = 512
DEC_PAGES = 8
N_SPLIT = 3


def _rmsnorm(x, g):
    r = lax.rsqrt(jnp.mean(x * x, axis=-1, keepdims=True) + RMS_EPS)
    return (x * r) * g


def _split_bf16(x):
    pieces = []
    rem = x
    for _ in range(N_SPLIT):
        p = rem.astype(BF16)
        pieces.append(p)
        rem = rem - p.astype(F32)
    return pieces


def _log_sigmoid(z):
    return jnp.minimum(z, 0.0) - jnp.log1p(jnp.exp(-jnp.abs(z)))


def _lane_broadcast_column(row):
    return jnp.broadcast_to(row, (LANES, row.shape[1])).T


def _params(*semantics):
    return pltpu.CompilerParams(dimension_semantics=semantics, vmem_limit_bytes=VMEM_LIMIT)


def _resident(shape):
    return pl.BlockSpec(shape, lambda *_: (0,) * len(shape), pipeline_mode=pl.Buffered(1))


def _pool_prompt_kernel(x_ref, halo_ref, g_ref, a_ref, buf_ref, hs_ref, *, tm, n_tiles):
    i = pl.program_id(1)
    g = g_ref[...]
    h = _rmsnorm(x_ref[...], g)
    hh = _rmsnorm(halo_ref[...], g)
    hs_ref[0:HALO, :] = jnp.where(i == 0, 0.0, hh)
    hs_ref[HALO:HALO + tm, :] = h
    pos = i * tm + lax.broadcasted_iota(jnp.int32, (tm, 1), 0)
    outs = []
    for gi, w in enumerate(POOL_WINDOWS):
        lo, hi = gi * POOL_GROUP, (gi + 1) * POOL_GROUP
        s = hs_ref[HALO:HALO + tm, lo:hi]
        for j in range(1, w):
            s = s + hs_ref[HALO - j:HALO - j + tm, lo:hi]
        cnt = jnp.minimum(w, pos + 1).astype(F32)
        outs.append(s / cnt)
    pooled = jnp.concatenate(outs, axis=-1)
    a_ref[...] = (pooled - h).astype(BF16)

    @pl.when(i == n_tiles - 1)
    def _():
        buf_ref[...] = hs_ref[tm:tm + HALO, :]


def _pool_prompt(x, g):
    b, t, d = x.shape
    tm = POOL_ROWS
    n_tiles = t // tm
    per = tm // HALO
    return pl.pallas_call(
        functools.partial(_pool_prompt_kernel, tm=tm, n_tiles=n_tiles),
        grid=(b, n_tiles),
        in_specs=[
            pl.BlockSpec((None, tm, d), lambda bi, i: (bi, i, 0)),
            pl.BlockSpec((None, HALO, d), lambda bi, i: (bi, jnp.maximum(i * per - 1, 0), 0)),
            pl.BlockSpec((1, d), lambda bi, i: (0, 0)),
        ],
        out_specs=[
            pl.BlockSpec((None, tm, d), lambda bi, i: (bi, i, 0)),
            pl.BlockSpec((None, HALO, d), lambda bi, i: (bi, 0, 0)),
        ],
        out_shape=[
            jax.ShapeDtypeStruct((b, t, d), BF16),
            jax.ShapeDtypeStruct((b, HALO, d), F32),
        ],
        scratch_shapes=[pltpu.VMEM((HALO + tm, d), F32)],
        compiler_params=_params("arbitrary", "arbitrary"),
        name="pool_prompt",
    )(x, x, g.reshape(1, d))


def _pool_sample_kernel(x_ref, st_ref, g_ref, a_ref, nst_ref, *, pos):
    h = _rmsnorm(x_ref[...], g_ref[...])
    outs = []
    for gi, w in enumerate(POOL_WINDOWS):
        lo, hi = gi * POOL_GROUP, (gi + 1) * POOL_GROUP
        s = h[:, lo:hi]
        for j in range(1, w):
            s = s + st_ref[POOL_BUF - j, :, lo:hi]
        outs.append(s / float(min(w, pos + 1)))
    pooled = jnp.concatenate(outs, axis=-1)
    a_ref[...] = (pooled - h).astype(BF16)
    for r in range(POOL_BUF - 1):
        nst_ref[r] = st_ref[r + 1]
    nst_ref[POOL_BUF - 1] = h


def _pool_sample(x, state_t, g, pos):
    bs, d = x.shape
    return pl.pallas_call(
        functools.partial(_pool_sample_kernel, pos=pos),
        out_shape=[
            jax.ShapeDtypeStruct((bs, d), BF16),
            jax.ShapeDtypeStruct((POOL_BUF, bs, d), F32),
        ],
        compiler_params=pltpu.CompilerParams(vmem_limit_bytes=VMEM_LIMIT),
        name="pool_sample",
    )(x, state_t, g.reshape(1, d))


def _mix_mlp_kernel(x_ref, a_ref, wm_ref, sc_ref, g_ref, wu_ref, wd_ref, o_ref):
    y = jnp.dot(a_ref[...], wm_ref[...], preferred_element_type=F32) * sc_ref[...]
    x1 = x_ref[...] + y
    h = _rmsnorm(x1, g_ref[...]).astype(BF16)
    acc = x1
    for c in range(D_FF // MLP_FF_CHUNK):
        cols = slice(c * MLP_FF_CHUNK, (c + 1) * MLP_FF_CHUNK)
        u = jnp.maximum(jnp.dot(h, wu_ref[:, cols], preferred_element_type=F32), 0.0)
        acc = acc + jnp.dot((u * u).astype(BF16), wd_ref[cols, :], preferred_element_type=F32)
    o_ref[...] = acc


def _mix_mlp(x, a, wm, sc, g, wu, wd):
    n, d = x.shape
    tm = min(MLP_ROWS, n)
    rows = lambda i: (i, 0)
    return pl.pallas_call(
        _mix_mlp_kernel,
        grid=(n // tm,),
        in_specs=[
            pl.BlockSpec((tm, d), rows),
            pl.BlockSpec((tm, d), rows),
            _resident((d, d)),
            _resident((1, d)),
            _resident((1, d)),
            _resident((d, D_FF)),
            _resident((D_FF, d)),
        ],
        out_specs=pl.BlockSpec((tm, d), rows),
        out_shape=jax.ShapeDtypeStruct((n, d), F32),
        compiler_params=_params("arbitrary"),
        name="mix_mlp",
    )(x, a, wm, sc.reshape(1, d), g.reshape(1, d), wu, wd)


def _final_norm_kernel(x_ref, g_ref, o_ref):
    o_ref[...] = _rmsnorm(x_ref[...], g_ref[...])


def _final_norm(x, g):
    n, d = x.shape
    tm = min(MLP_ROWS, n)
    return pl.pallas_call(
        _final_norm_kernel,
        grid=(n // tm,),
        in_specs=[pl.BlockSpec((tm, d), lambda i: (i, 0)), pl.BlockSpec((1, d), lambda i: (0, 0))],
        out_specs=pl.BlockSpec((tm, d), lambda i: (i, 0)),
        out_shape=jax.ShapeDtypeStruct((n, d), F32),
        compiler_params=_params("arbitrary"),
        name="final_norm",
    )(x, g.reshape(1, d))


def _aug_constants():
    width = N_HEADS * LANES
    pq = np.zeros((LANES, width), np.float32)
    pk = np.zeros((LANES, width), np.float32)
    oq = np.zeros((1, width), np.float32)
    ok = np.zeros((1, width), np.float32)
    for h in range(N_HEADS):
        base = h * LANES + (HEAD_DIM if h % 2 == 0 else 0)
        for p in range(N_SPLIT):
            oq[0, base + p] = 1.0
            pq[p * N_HEADS + h, base + N_SPLIT + p] = 1.0
            pk[p * N_HEADS + h, base + p] = 1.0
            ok[0, base + N_SPLIT + p] = 1.0
    return (jnp.asarray(pq, BF16), jnp.asarray(pk, BF16), jnp.asarray(oq), jnp.asarray(ok))


def _gate_logits(z, bf_ref):
    z = z + bf_ref[...]
    lane = lax.broadcasted_iota(jnp.int32, z.shape, 1)
    return jnp.where(lane < N_HEADS, _log_sigmoid(z), 0.0)


def _fox_proj_prompt_kernel(x_ref, g_ref, w_ref, wvt_ref, bf_ref, pq_ref, pk_ref, oq_ref, ok_ref,
                            kt_ref, vt_ref, lft_ref, qa_ref, ka_ref, vtb_ref, carry_ref, *, tm):
    i = pl.program_id(1)

    @pl.when(i == 0)
    def _():
        carry_ref[...] = jnp.zeros_like(carry_ref)

    h = _rmsnorm(x_ref[...], g_ref[...]).astype(BF16)
    qk = jnp.dot(h, w_ref[...], preferred_element_type=F32)
    q = qk[:, 0:D_MODEL] * Q_SCALE
    k = qk[:, D_MODEL:2 * D_MODEL]
    lf = _gate_logits(qk[:, 2 * D_MODEL:2 * D_MODEL + LANES], bf_ref)
    vt = lax.dot_general(wvt_ref[...], h, (((1,), (1,)), ((), ())), preferred_element_type=F32)
    kt_ref[...] = k.T
    vt_ref[...] = vt
    vtb_ref[...] = vt.astype(BF16)
    lft_ref[...] = lf.T[0:N_HEADS, :]

    row = lax.broadcasted_iota(jnp.int32, (tm, tm), 0)
    col = lax.broadcasted_iota(jnp.int32, (tm, tm), 1)
    tri = (row >= col).astype(BF16)
    f = carry_ref[...]
    for piece in _split_bf16(lf * LOG2E):
        f = f + jnp.dot(tri, piece, preferred_element_type=F32)
    carry_ref[...] = f[tm - 1:tm, :]

    packed = None
    for p, piece in enumerate(_split_bf16(f)):
        shifted = piece.astype(F32)
        if p:
            shifted = pltpu.roll(shifted, p * N_HEADS, axis=1)
        packed = shifted if packed is None else packed + shifted
    packed = packed.astype(BF16)
    aug_q = jnp.dot(packed, pq_ref[...], preferred_element_type=F32) + oq_ref[...]
    aug_k = jnp.dot(-packed, pk_ref[...], preferred_element_type=F32) + ok_ref[...]

    lane = lax.broadcasted_iota(jnp.int32, (tm, LANES), 1)
    for hd in range(N_HEADS):
        slab = slice((hd // 2) * LANES, (hd // 2 + 1) * LANES)
        own = slice(hd * LANES, (hd + 1) * LANES)
        keep = (lane < HEAD_DIM) if hd % 2 == 0 else (lane >= HEAD_DIM)
        qa_ref[hd] = jnp.where(keep, q[:, slab], aug_q[:, own]).astype(BF16)
        ka_ref[hd] = jnp.where(keep, k[:, slab], aug_k[:, own]).astype(BF16)


def _fox_proj_prompt(x, g, wqk, wvt, bfp):
    b, t, d = x.shape
    tm = PROJ_ROWS
    n_tiles = t // tm
    pq, pk, oq, ok = _aug_constants()
    rows = lambda bi, i: (bi, i, 0)
    cols = lambda bi, i: (bi, 0, i)
    return pl.pallas_call(
        functools.partial(_fox_proj_prompt_kernel, tm=tm),
        grid=(b, n_tiles),
        in_specs=[
            pl.BlockSpec((None, tm, d), rows),
            _resident((1, d)),
            _resident(wqk.shape),
            _resident(wvt.shape),
            _resident((1, LANES)),
            _resident(pq.shape),
            _resident(pk.shape),
            _resident(oq.shape),
            _resident(ok.shape),
        ],
        out_specs=[
            pl.BlockSpec((None, d, tm), cols),
            pl.BlockSpec((None, d, tm), cols),
            pl.BlockSpec((None, N_HEADS, tm), cols),
            pl.BlockSpec((None, N_HEADS, tm, LANES), lambda bi, i: (bi, 0, i, 0)),
            pl.BlockSpec((None, N_HEADS, tm, LANES), lambda bi, i: (bi, 0, i, 0)),
            pl.BlockSpec((None, None, d, tm), lambda bi, i: (bi, i, 0, 0)),
        ],
        out_shape=[
            jax.ShapeDtypeStruct((b, d, t), F32),
            jax.ShapeDtypeStruct((b, d, t), F32),
            jax.ShapeDtypeStruct((b, N_HEADS, t), F32),
            jax.ShapeDtypeStruct((b, N_HEADS, t, LANES), BF16),
            jax.ShapeDtypeStruct((b, N_HEADS, t, LANES), BF16),
            jax.ShapeDtypeStruct((b, n_tiles, d, tm), BF16),
        ],
        scratch_shapes=[pltpu.VMEM((1, LANES), F32)],
        compiler_params=_params("arbitrary", "arbitrary"),
        name="fox_proj_prompt",
    )(x, g.reshape(1, d), wqk, wvt, bfp, pq, pk, oq, ok)


def _fox_proj_sample_kernel(x_ref, g_ref, w_ref, wvt_ref, bf_ref, q_ref, k_ref, v_ref, lf_ref):
    h = _rmsnorm(x_ref[...], g_ref[...]).astype(BF16)
    qk = jnp.dot(h, w_ref[...], preferred_element_type=F32)
    q_ref[...] = qk[:, 0:D_MODEL] * Q_SCALE
    k_ref[...] = qk[:, D_MODEL:2 * D_MODEL]
    v_ref[...] = lax.dot_general(h, wvt_ref[...], (((1,), (1,)), ((), ())), preferred_element_type=F32)
    lf_ref[...] = _gate_logits(qk[:, 2 * D_MODEL:2 * D_MODEL + LANES], bf_ref)[:, 0:N_HEADS]


def _fox_proj_sample(x, g, wqk, wvt, bfp):
    bs, d = x.shape
    return pl.pallas_call(
        _fox_proj_sample_kernel,
        out_shape=[
            jax.ShapeDtypeStruct((bs, d), F32),
            jax.ShapeDtypeStruct((bs, d), F32),
            jax.ShapeDtypeStruct((bs, d), F32),
            jax.ShapeDtypeStruct((bs, N_HEADS), F32),
        ],
        compiler_params=pltpu.CompilerParams(vmem_limit_bytes=VMEM_LIMIT),
        name="fox_proj_sample",
    )(x, g.reshape(1, d), wqk, wvt, bfp)


def _attn_prompt_kernel(q_ref, k_ref, vt_ref, o_ref, m_ref, l_ref, acc_ref, *, tq, tk):
    i = pl.program_id(2)
    n_diag = tq // tk
    n_full = i * n_diag
    outs = []
    for hh in range(2):
        q = q_ref[hh]
        m_ref[...] = jnp.full_like(m_ref, -jnp.inf)
        l_ref[...] = jnp.zeros_like(l_ref)
        acc_ref[...] = jnp.zeros_like(acc_ref)

        def step(jb, masked, hh=hh, q=q):
            kb = k_ref[hh, pl.ds(pl.multiple_of(jb * tk, tk), tk), :]
            s = lax.dot_general(kb, q, (((1,), (1,)), ((), ())), preferred_element_type=F32)
            if masked:
                kpos = jb * tk + lax.broadcasted_iota(jnp.int32, (tk, tq), 0)
                qpos = i * tq + lax.broadcasted_iota(jnp.int32, (tk, tq), 1)
                s = jnp.where(kpos <= qpos, s, -jnp.inf)
            m_old = m_ref[...]
            m_new = jnp.maximum(m_old, jnp.max(s, axis=0, keepdims=True))
            p = jnp.exp2(s - m_new)
            alpha = jnp.exp2(m_old - m_new)
            l_ref[...] = alpha * l_ref[...] + jnp.sum(p, axis=0, keepdims=True)
            vt = vt_ref[jb, hh * HEAD_DIM:(hh + 1) * HEAD_DIM, :]
            acc_ref[...] = alpha * acc_ref[...] + jnp.dot(vt, p.astype(BF16), preferred_element_type=F32)
            m_ref[...] = m_new

        def body(jb, carry):
            step(jb, False)
            return carry

        lax.fori_loop(0, n_full, body, 0)
        for dj in range(n_diag):
            step(n_full + dj, True)
        outs.append(acc_ref[...] / l_ref[...])
    o_ref[...] = jnp.concatenate(outs, axis=0).T.astype(BF16)


def _attn_prompt(qa, ka, vtb):
    b, nh, t, _ = qa.shape
    n_kv, d, tk = vtb.shape[1], vtb.shape[2], vtb.shape[3]
    tq = ATTN_Q
    return pl.pallas_call(
        functools.partial(_attn_prompt_kernel, tq=tq, tk=tk),
        grid=(b, nh // 2, t // tq),
        in_specs=[
            pl.BlockSpec((None, 2, tq, LANES), lambda bi, j, i: (bi, j, i, 0)),
            pl.BlockSpec((None, 2, t, LANES), lambda bi, j, i: (bi, j, 0, 0)),
            pl.BlockSpec((None, n_kv, LANES, tk), lambda bi, j, i: (bi, 0, j, 0)),
        ],
        out_specs=pl.BlockSpec((None, tq, LANES), lambda bi, j, i: (bi, i, j)),
        out_shape=jax.ShapeDtypeStruct((b, t, d), BF16),
        scratch_shapes=[
            pltpu.VMEM((1, tq), F32),
            pltpu.VMEM((1, tq), F32),
            pltpu.VMEM((HEAD_DIM, tq), F32),
        ],
        compiler_params=_params("arbitrary", "arbitrary", "arbitrary"),
        name="attn_prompt",
    )(qa, ka, vtb)


def _head_sums(x):
    return jnp.sum(x.reshape(N_HEADS, HEAD_DIM, x.shape[1]), axis=1)


def _attn_sample_kernel(pt_ref, q_ref, kn_ref, vn_ref, lfn_ref, *refs, n_pages, n_chunks):
    del pt_ref
    k_refs = refs[0:n_pages]
    v_refs = refs[n_pages:2 * n_pages]
    lf_refs = refs[2 * n_pages:3 * n_pages]
    o_ref, qb_ref, m_ref, l_ref, acc_ref, fc_ref = refs[3 * n_pages:]
    c = pl.program_id(1)

    @pl.when(c == 0)
    def _():
        qb_ref[...] = _lane_broadcast_column(q_ref[...])
        m_ref[...] = jnp.full_like(m_ref, -jnp.inf)
        l_ref[...] = jnp.zeros_like(l_ref)
        acc_ref[...] = jnp.zeros_like(acc_ref)
        fc_ref[...] = jnp.zeros_like(fc_ref)

    qb = qb_ref[...]
    row = lax.broadcasted_iota(jnp.int32, (PAGE_SIZE, PAGE_SIZE), 0)
    col = lax.broadcasted_iota(jnp.int32, (PAGE_SIZE, PAGE_SIZE), 1)
    upper = (row <= col).astype(BF16)

    f_run = fc_ref[...]
    zs = []
    for r_ in range(n_pages):
        s = _head_sums(k_refs[r_][...] * qb)
        f = f_run
        for piece in _split_bf16(lf_refs[r_][...] * LOG2E):
            f = f + jnp.dot(piece, upper, preferred_element_type=F32)
        f_run = f[:, PAGE_SIZE - 1:PAGE_SIZE]
        zs.append(s - f)
    fc_ref[...] = f_run
    z = jnp.concatenate(zs, axis=1)

    m_old = m_ref[...]
    m_new = jnp.maximum(m_old, jnp.max(z, axis=1, keepdims=True))
    p = jnp.exp2(z - m_new)
    alpha = jnp.exp2(m_old - m_new)
    l_new = alpha * l_ref[...] + jnp.sum(p, axis=1, keepdims=True)
    for hd in range(N_HEADS):
        rows = slice(hd * HEAD_DIM, (hd + 1) * HEAD_DIM)
        a = acc_ref[rows, :] * alpha[hd:hd + 1, :]
        for r_ in range(n_pages):
            a = a + v_refs[r_][rows, :] * p[hd:hd + 1, r_ * PAGE_SIZE:(r_ + 1) * PAGE_SIZE]
        acc_ref[rows, :] = a
    m_ref[...] = m_new
    l_ref[...] = l_new

    @pl.when(c == n_chunks - 1)
    def _():
        s_new = _head_sums(_lane_broadcast_column(kn_ref[...]) * qb)[:, 0:1]
        z_new = s_new - (f_run + lfn_ref[...] * LOG2E)
        m_fin = jnp.maximum(m_new, z_new)
        p_new = jnp.exp2(z_new - m_fin)
        a_fin = jnp.exp2(m_new - m_fin)
        inv_l = 1.0 / (a_fin * l_new + p_new)
        vnb = _lane_broadcast_column(vn_ref[...])
        lane0 = lax.broadcasted_iota(jnp.int32, (HEAD_DIM, LANES), 1) == 0
        for hd in range(N_HEADS):
            rows = slice(hd * HEAD_DIM, (hd + 1) * HEAD_DIM)
            a = acc_ref[rows, :] * a_fin[hd:hd + 1, :]
            a = a + jnp.where(lane0, vnb[rows, :] * p_new[hd:hd + 1, :], 0.0)
            acc_ref[rows, :] = a * inv_l[hd:hd + 1, :]
        o_ref[...] = jnp.sum(acc_ref[...].T, axis=0, keepdims=True).astype(BF16)


def _attn_sample(page_table, q, kn, vn, lfn, cache_kt, cache_vt, cache_lft, layer):
    bs, d = q.shape
    n_log = page_table.shape[1]
    n_pages = DEC_PAGES
    n_chunks = n_log // n_pages

    def page_map(r_):
        return lambda b, c, pt: (layer, pt[b, c * n_pages + r_], 0, 0)

    tok = lambda b, c, pt: (b, 0, 0)
    in_specs = [pl.BlockSpec((None, 1, d), tok)] * 3 + [pl.BlockSpec((None, N_HEADS, 1), tok)]
    in_specs += [pl.BlockSpec((None, None, d, PAGE_SIZE), page_map(r_)) for r_ in range(n_pages)]
    in_specs += [pl.BlockSpec((None, None, d, PAGE_SIZE), page_map(r_)) for r_ in range(n_pages)]
    in_specs += [pl.BlockSpec((None, None, N_HEADS, PAGE_SIZE), page_map(r_)) for r_ in range(n_pages)]
    grid_spec = pltpu.PrefetchScalarGridSpec(
        num_scalar_prefetch=1,
        grid=(bs, n_chunks),
        in_specs=in_specs,
        out_specs=pl.BlockSpec((None, 1, d), tok),
        scratch_shapes=[
            pltpu.VMEM((d, LANES), F32),
            pltpu.VMEM((N_HEADS, 1), F32),
            pltpu.VMEM((N_HEADS, 1), F32),
            pltpu.VMEM((d, LANES), F32),
            pltpu.VMEM((N_HEADS, 1), F32),
        ],
    )
    out = pl.pallas_call(
        functools.partial(_attn_sample_kernel, n_pages=n_pages, n_chunks=n_chunks),
        grid_spec=grid_spec,
        out_shape=jax.ShapeDtypeStruct((bs, 1, d), BF16),
        compiler_params=_params("arbitrary", "arbitrary"),
        name="attn_sample",
    )(page_table, q.reshape(bs, 1, d), kn.reshape(bs, 1, d), vn.reshape(bs, 1, d),
      lfn.reshape(bs, N_HEADS, 1), *([cache_kt] * n_pages), *([cache_vt] * n_pages),
      *([cache_lft] * n_pages))
    return out.reshape(bs, d)


def _feature_major_pages(cache):
    if cache.ndim == 5:
        n_layers, n_phys, page, nh, hd = cache.shape
        return jnp.transpose(cache, (0, 1, 3, 4, 2)).reshape(n_layers, n_phys, nh * hd, page)
    return jnp.transpose(cache, (0, 1, 3, 2))


def kernel(x_prompt, x_sample, cache_k, cache_v, cache_logf, state_pool, page_table, norm_mix, norm_mlp,
           norm_final, pool_w, pool_scale, w_qkv, w_f, b_f, w_o, w_up, w_down):
    b, t, d = x_prompt.shape
    bs, ts, _ = x_sample.shape
    assert ts == 1 and d == D_MODEL
    depth = norm_mix.shape[0]
    past = page_table.shape[1] * PAGE_SIZE

    ckt = _feature_major_pages(cache_k)
    cvt = _feature_major_pages(cache_v)
    clft = _feature_major_pages(cache_logf)
    ones = jnp.ones((d,), F32)

    xp = x_prompt.reshape(b * t, d)
    xs = x_sample.reshape(bs, d)
    kp, vp, lp, pp = [], [], [], []
    ksl, vsl, lsl, psl = [], [], [], []
    for i in range(depth):
        j = i // 2
        if i % 2 == 0:
            a_p, buf = _pool_prompt(xp.reshape(b, t, d), norm_mix[i])
            a_p = a_p.reshape(b * t, d)
            pp.append(buf[:, HALO - POOL_BUF:, :])
            a_s, nst = _pool_sample(xs, jnp.transpose(state_pool[j], (1, 0, 2)), norm_mix[i], past)
            psl.append(jnp.transpose(nst, (1, 0, 2)))
            wm = jax.scipy.linalg.block_diag(*[pool_w[j, gi] for gi in range(len(POOL_WINDOWS))]).astype(BF16)
            sc = pool_scale[j]
        else:
            wqk = jnp.concatenate(
                [w_qkv[j, :, 0:2 * d], jnp.pad(w_f[j], ((0, 0), (0, LANES - N_HEADS)))], axis=1).astype(BF16)
            wvt = w_qkv[j, :, 2 * d:3 * d].T.astype(BF16)
            bfp = jnp.pad(b_f[j], (0, LANES - N_HEADS)).reshape(1, LANES)
            kt, vt, lft, qa, ka, vtb = _fox_proj_prompt(xp.reshape(b, t, d), norm_mix[i], wqk, wvt, bfp)
            a_p = _attn_prompt(qa, ka, vtb).reshape(b * t, d)
            kp.append(jnp.transpose(kt.reshape(b, N_HEADS, HEAD_DIM, t), (0, 3, 1, 2)))
            vp.append(jnp.transpose(vt.reshape(b, N_HEADS, HEAD_DIM, t), (0, 3, 1, 2)))
            lp.append(jnp.transpose(lft, (0, 2, 1)))
            q2, k2, v2, l2 = _fox_proj_sample(xs, norm_mix[i], wqk, wvt, bfp)
            a_s = _attn_sample(page_table, q2, k2, v2, l2, ckt, cvt, clft, j)
            ksl.append(k2.reshape(bs, 1, N_HEADS, HEAD_DIM))
            vsl.append(v2.reshape(bs, 1, N_HEADS, HEAD_DIM))
            lsl.append(l2.reshape(bs, 1, N_HEADS))
            wm = w_o[j].astype(BF16)
            sc = ones
        wu = w_up[i].astype(BF16)
        wd = w_down[i].astype(BF16)
        xp = _mix_mlp(xp, a_p, wm, sc, norm_mlp[i], wu, wd)
        xs = _mix_mlp(xs, a_s, wm, sc, norm_mlp[i], wu, wd)
    y_prompt = _final_norm(xp, norm_final).reshape(b, t, d)
    y_sample = _final_norm(xs, norm_final).reshape(bs, 1, d)
    return (y_prompt, y_sample,
            jnp.stack(kp), jnp.stack(vp), jnp.stack(lp), jnp.stack(pp),
            jnp.stack(ksl), jnp.stack(vsl), jnp.stack(lsl), jnp.stack(psl))
```

```python
import functools

import numpy as np
import jax
import jax.numpy as jnp
from jax import lax
from jax.experimental import pallas as pl
from jax.experimental.pallas import tpu as pltpu

F32 = jnp.float32
BF16 = jnp.bfloat16

D_MODEL = 1024
N_HEADS = 16
HEAD_DIM = 64
D_FF = 4 * D_MODEL
POOL_WINDOWS = (2, 4, 8, 16)
POOL_GROUP = D_MODEL // len(POOL_WINDOWS)
POOL_BUF = max(POOL_WINDOWS) - 1
PAGE_SIZE = 128
RMS_EPS = 1e-6

LANES = 128
HALO = 16
VMEM_LIMIT = 56 * 1024 * 1024

LOG2E = 1.4426950408889634
Q_SCALE = HEAD_DIM ** -0.5 * LOG2E

MLP_ROWS = 512
MLP_FF_CHUNK = 1024
POOL_ROWS = 512
PROJ_ROWS = 256
ATTN_Q = 512
ATTN_HEADS = 4
DEC_PAGES = 8
N_SPLIT = 3


def _rmsnorm(x, g):
    r = lax.rsqrt(jnp.mean(x * x, axis=-1, keepdims=True) + RMS_EPS)
    return (x * r) * g


def _split_bf16(x):
    pieces = []
    rem = x
    for _ in range(N_SPLIT):
        p = rem.astype(BF16)
        pieces.append(p)
        rem = rem - p.astype(F32)
    return pieces


def _log_sigmoid(z):
    return jnp.minimum(z, 0.0) - jnp.log1p(jnp.exp(-jnp.abs(z)))


def _lane_broadcast_column(row):
    return jnp.broadcast_to(row, (LANES, row.shape[1])).T


def _params(*semantics):
    return pltpu.CompilerParams(dimension_semantics=semantics, vmem_limit_bytes=VMEM_LIMIT)


def _resident(shape):
    return pl.BlockSpec(shape, lambda *_: (0,) * len(shape), pipeline_mode=pl.Buffered(1))


def _pool_prompt_kernel(x_ref, halo_ref, g_ref, a_ref, buf_ref, *lvl_refs, tm, n_tiles):
    i = pl.program_id(1)
    g = g_ref[...]
    h = _rmsnorm(x_ref[...], g)
    hh = _rmsnorm(halo_ref[...], g)
    top = 2 * HALO
    end = top + tm
    e0 = lvl_refs[0]
    e0[0:HALO, :] = jnp.zeros((HALO, h.shape[1]), F32)
    e0[HALO:top, :] = jnp.where(i == 0, 0.0, hh)
    e0[top:end, :] = h
    pos = i * tm + lax.broadcasted_iota(jnp.int32, (tm, 1), 0)
    outs = []
    prev = e0
    for lvl, w in enumerate(POOL_WINDOWS):
        first = 8 * (lvl + 1)
        shift = w // 2
        drop = POOL_GROUP if lvl else 0
        if lvl + 1 < len(POOL_WINDOWS):
            cur = lvl_refs[lvl + 1]
            cur[first:end, :] = prev[first:end, drop:] + prev[first - shift:end - shift, drop:]
            s = cur[top:end, 0:POOL_GROUP]
            prev = cur
        else:
            s = prev[top:end, drop:] + prev[top - shift:end - shift, drop:]
        cnt = jnp.minimum(w, pos + 1).astype(F32)
        outs.append(s / cnt)
    pooled = jnp.concatenate(outs, axis=-1)
    a_ref[...] = (pooled - h).astype(BF16)

    @pl.when(i == n_tiles - 1)
    def _():
        buf_ref[...] = e0[end - HALO:end, :]


def _pool_prompt(x, g):
    b, t, d = x.shape
    tm = POOL_ROWS
    n_tiles = t // tm
    per = tm // HALO
    return pl.pallas_call(
        functools.partial(_pool_prompt_kernel, tm=tm, n_tiles=n_tiles),
        grid=(b, n_tiles),
        in_specs=[
            pl.BlockSpec((None, tm, d), lambda bi, i: (bi, i, 0)),
            pl.BlockSpec((None, HALO, d), lambda bi, i: (bi, jnp.maximum(i * per - 1, 0), 0)),
            pl.BlockSpec((1, d), lambda bi, i: (0, 0)),
        ],
        out_specs=[
            pl.BlockSpec((None, tm, d), lambda bi, i: (bi, i, 0)),
            pl.BlockSpec((None, HALO, d), lambda bi, i: (bi, 0, 0)),
        ],
        out_shape=[
            jax.ShapeDtypeStruct((b, t, d), BF16),
            jax.ShapeDtypeStruct((b, HALO, d), F32),
        ],
        scratch_shapes=[pltpu.VMEM((2 * HALO + tm, d - max(lvl - 1, 0) * POOL_GROUP), F32)
                        for lvl in range(len(POOL_WINDOWS))],
        compiler_params=_params("arbitrary", "arbitrary"),
        name="pool_prompt",
    )(x, x, g.reshape(1, d))


def _pool_sample_kernel(x_ref, st_ref, g_ref, a_ref, nst_ref, *, pos):
    h = _rmsnorm(x_ref[...], g_ref[...])
    outs = []
    for gi, w in enumerate(POOL_WINDOWS):
        lo, hi = gi * POOL_GROUP, (gi + 1) * POOL_GROUP
        s = h[:, lo:hi]
        for j in range(1, w):
            s = s + st_ref[POOL_BUF - j, :, lo:hi]
        outs.append(s / float(min(w, pos + 1)))
    pooled = jnp.concatenate(outs, axis=-1)
    a_ref[...] = (pooled - h).astype(BF16)
    for r in range(POOL_BUF - 1):
        nst_ref[r] = st_ref[r + 1]
    nst_ref[POOL_BUF - 1] = h


def _pool_sample(x, state_t, g, pos):
    bs, d = x.shape
    return pl.pallas_call(
        functools.partial(_pool_sample_kernel, pos=pos),
        out_shape=[
            jax.ShapeDtypeStruct((bs, d), BF16),
            jax.ShapeDtypeStruct((POOL_BUF, bs, d), F32),
        ],
        compiler_params=pltpu.CompilerParams(vmem_limit_bytes=VMEM_LIMIT),
        name="pool_sample",
    )(x, state_t, g.reshape(1, d))


def _mix_mlp_kernel(x_ref, a_ref, wm_ref, sc_ref, g_ref, wu_ref, wd_ref, o_ref):
    if len(wm_ref.shape) == 3:
        gw = wm_ref.shape[1]
        y = jnp.concatenate(
            [jnp.dot(a_ref[:, gi * gw:(gi + 1) * gw], wm_ref[gi], preferred_element_type=F32)
             for gi in range(wm_ref.shape[0])], axis=-1)
    else:
        y = jnp.dot(a_ref[...], wm_ref[...], preferred_element_type=F32)
    x1 = x_ref[...] + y * sc_ref[...]
    h = _rmsnorm(x1, g_ref[...]).astype(BF16)
    acc = x1
    for c in range(D_FF // MLP_FF_CHUNK):
        cols = slice(c * MLP_FF_CHUNK, (c + 1) * MLP_FF_CHUNK)
        u = jnp.maximum(jnp.dot(h, wu_ref[:, cols], preferred_element_type=F32), 0.0)
        acc = acc + jnp.dot((u * u).astype(BF16), wd_ref[cols, :], preferred_element_type=F32)
    o_ref[...] = acc


def _mix_mlp(x, a, wm, sc, g, wu, wd):
    n, d = x.shape
    tm = min(MLP_ROWS, n)
    rows = lambda i: (i, 0)
    return pl.pallas_call(
        _mix_mlp_kernel,
        grid=(n // tm,),
        in_specs=[
            pl.BlockSpec((tm, d), rows),
            pl.BlockSpec((tm, d), rows),
            _resident(wm.shape),
            _resident((1, d)),
            _resident((1, d)),
            _resident((d, D_FF)),
            _resident((D_FF, d)),
        ],
        out_specs=pl.BlockSpec((tm, d), rows),
        out_shape=jax.ShapeDtypeStruct((n, d), F32),
        compiler_params=_params("arbitrary"),
        name="mix_mlp",
    )(x, a, wm, sc.reshape(1, d), g.reshape(1, d), wu, wd)


def _final_norm_kernel(x_ref, g_ref, o_ref):
    o_ref[...] = _rmsnorm(x_ref[...], g_ref[...])


def _final_norm(x, g):
    n, d = x.shape
    tm = min(MLP_ROWS, n)
    return pl.pallas_call(
        _final_norm_kernel,
        grid=(n // tm,),
        in_specs=[pl.BlockSpec((tm, d), lambda i: (i, 0)), pl.BlockSpec((1, d), lambda i: (0, 0))],
        out_specs=pl.BlockSpec((tm, d), lambda i: (i, 0)),
        out_shape=jax.ShapeDtypeStruct((n, d), F32),
        compiler_params=_params("arbitrary"),
        name="final_norm",
    )(x, g.reshape(1, d))


def _aug_constants():
    width = (N_HEADS // 2) * LANES
    pq = np.zeros((LANES, width), np.float32)
    pk = np.zeros((LANES, width), np.float32)
    oq = np.zeros((1, width), np.float32)
    ok = np.zeros((1, width), np.float32)
    for h in range(N_HEADS):
        base = (h // 2) * LANES + (HEAD_DIM if h % 2 == 0 else 0)
        for p in range(N_SPLIT):
            oq[0, base + p] = 1.0
            pq[p * N_HEADS + h, base + N_SPLIT + p] = 1.0
            pk[p * N_HEADS + h, base + p] = 1.0
            ok[0, base + N_SPLIT + p] = 1.0
    return (jnp.asarray(pq, BF16), jnp.asarray(pk, BF16), jnp.asarray(oq), jnp.asarray(ok))


def _gate_logits(z, bf_ref):
    z = z + bf_ref[...]
    lane = lax.broadcasted_iota(jnp.int32, z.shape, 1)
    return jnp.where(lane < N_HEADS, _log_sigmoid(z), 0.0)


def _fox_proj_prompt_kernel(x_ref, g_ref, w_ref, wvt_ref, bf_ref, pq_ref, pk_ref, oq_ref, ok_ref,
                            kt_ref, vt_ref, lft_ref, qa_ref, ka_ref, vtb_ref, carry_ref, *, tm):
    i = pl.program_id(1)

    @pl.when(i == 0)
    def _():
        carry_ref[...] = jnp.zeros_like(carry_ref)

    h = _rmsnorm(x_ref[...], g_ref[...]).astype(BF16)
    qk = jnp.dot(h, w_ref[...], preferred_element_type=F32)
    q = qk[:, 0:D_MODEL] * Q_SCALE
    k = qk[:, D_MODEL:2 * D_MODEL]
    lf = _gate_logits(qk[:, 2 * D_MODEL:2 * D_MODEL + LANES], bf_ref)
    vt = lax.dot_general(wvt_ref[...], h, (((1,), (1,)), ((), ())), preferred_element_type=F32)
    kt_ref[...] = k.T
    vt_ref[...] = vt
    vtb_ref[...] = vt.astype(BF16)
    lft_ref[...] = lf.T[0:N_HEADS, :]

    row = lax.broadcasted_iota(jnp.int32, (tm, tm), 0)
    col = lax.broadcasted_iota(jnp.int32, (tm, tm), 1)
    tri = (row >= col).astype(BF16)
    f = carry_ref[...]
    for piece in _split_bf16(lf * LOG2E):
        f = f + jnp.dot(tri, piece, preferred_element_type=F32)
    carry_ref[...] = f[tm - 1:tm, :]

    packed = None
    for p, piece in enumerate(_split_bf16(f)):
        shifted = piece.astype(F32)
        if p:
            shifted = pltpu.roll(shifted, p * N_HEADS, axis=1)
        packed = shifted if packed is None else packed + shifted
    packed = packed.astype(BF16)
    aug_q = jnp.dot(packed, pq_ref[...], preferred_element_type=F32) + oq_ref[...]
    aug_k = jnp.dot(-packed, pk_ref[...], preferred_element_type=F32) + ok_ref[...]

    lane = lax.broadcasted_iota(jnp.int32, (tm, LANES), 1)
    for hd in range(N_HEADS):
        slab = slice((hd // 2) * LANES, (hd // 2 + 1) * LANES)
        keep = (lane < HEAD_DIM) if hd % 2 == 0 else (lane >= HEAD_DIM)
        qa_ref[hd] = jnp.where(keep, q[:, slab], aug_q[:, slab]).astype(BF16)
        ka_ref[hd] = jnp.where(keep, k[:, slab], aug_k[:, slab]).astype(BF16)


def _fox_proj_prompt(x, g, wqk, wvt, bfp):
    b, t, d = x.shape
    tm = PROJ_ROWS
    n_tiles = t // tm
    pq, pk, oq, ok = _aug_constants()
    rows = lambda bi, i: (bi, i, 0)
    cols = lambda bi, i: (bi, 0, i)
    return pl.pallas_call(
        functools.partial(_fox_proj_prompt_kernel, tm=tm),
        grid=(b, n_tiles),
        in_specs=[
            pl.BlockSpec((None, tm, d), rows),
            _resident((1, d)),
            _resident(wqk.shape),
            _resident(wvt.shape),
            _resident((1, LANES)),
            _resident(pq.shape),
            _resident(pk.shape),
            _resident(oq.shape),
            _resident(ok.shape),
        ],
        out_specs=[
            pl.BlockSpec((None, d, tm), cols),
            pl.BlockSpec((None, d, tm), cols),
            pl.BlockSpec((None, N_HEADS, tm), cols),
            pl.BlockSpec((None, N_HEADS, tm, LANES), lambda bi, i: (bi, 0, i, 0)),
            pl.BlockSpec((None, N_HEADS, tm, LANES), lambda bi, i: (bi, 0, i, 0)),
            pl.BlockSpec((None, None, d, tm), lambda bi, i: (bi, i, 0, 0)),
        ],
        out_shape=[
            jax.ShapeDtypeStruct((b, d, t), F32),
            jax.ShapeDtypeStruct((b, d, t), F32),
            jax.ShapeDtypeStruct((b, N_HEADS, t), F32),
            jax.ShapeDtypeStruct((b, N_HEADS, t, LANES), BF16),
            jax.ShapeDtypeStruct((b, N_HEADS, t, LANES), BF16),
            jax.ShapeDtypeStruct((b, n_tiles, d, tm), BF16),
        ],
        scratch_shapes=[pltpu.VMEM((1, LANES), F32)],
        compiler_params=_params("arbitrary", "arbitrary"),
        name="fox_proj_prompt",
    )(x, g.reshape(1, d), wqk, wvt, bfp, pq, pk, oq, ok)


def _fox_proj_sample_kernel(x_ref, g_ref, w_ref, wvt_ref, bf_ref, q_ref, k_ref, v_ref, lf_ref):
    h = _rmsnorm(x_ref[...], g_ref[...]).astype(BF16)
    qk = jnp.dot(h, w_ref[...], preferred_element_type=F32)
    q_ref[...] = qk[:, 0:D_MODEL] * Q_SCALE
    k_ref[...] = qk[:, D_MODEL:2 * D_MODEL]
    v_ref[...] = lax.dot_general(h, wvt_ref[...], (((1,), (1,)), ((), ())), preferred_element_type=F32)
    lf_ref[...] = _gate_logits(qk[:, 2 * D_MODEL:2 * D_MODEL + LANES], bf_ref)[:, 0:N_HEADS]


def _fox_proj_sample(x, g, wqk, wvt, bfp):
    bs, d = x.shape
    return pl.pallas_call(
        _fox_proj_sample_kernel,
        out_shape=[
            jax.ShapeDtypeStruct((bs, d), F32),
            jax.ShapeDtypeStruct((bs, d), F32),
            jax.ShapeDtypeStruct((bs, d), F32),
            jax.ShapeDtypeStruct((bs, N_HEADS), F32),
        ],
        compiler_params=pltpu.CompilerParams(vmem_limit_bytes=VMEM_LIMIT),
        name="fox_proj_sample",
    )(x, g.reshape(1, d), wqk, wvt, bfp)


def _attn_prompt_kernel(q_ref, k_ref, vt_ref, o_ref, m_ref, l_ref, acc_ref, *, tq, tk, nh):
    i = pl.program_id(2)
    n_sub = tq // tk
    m_ref[...] = jnp.full_like(m_ref, -jnp.inf)
    l_ref[...] = jnp.zeros_like(l_ref)
    acc_ref[...] = jnp.zeros_like(acc_ref)

    def scores(hh, it):
        kb = k_ref[hh, pl.ds(pl.multiple_of(it * tq, tq), tq), :]
        return lax.dot_general(kb, q_ref[hh], (((1,), (1,)), ((), ())), preferred_element_type=F32)

    def update(hh, it, s):
        m_old = m_ref[hh]
        m_new = jnp.maximum(m_old, jnp.max(s, axis=0, keepdims=True))
        p = jnp.exp2(s - m_new)
        alpha = jnp.exp2(m_old - m_new)
        l_ref[hh] = alpha * l_ref[hh] + jnp.sum(p, axis=0, keepdims=True)
        pb = p.astype(BF16)
        acc = alpha * acc_ref[hh]
        for c in range(n_sub):
            vt = vt_ref[it * n_sub + c, hh * HEAD_DIM:(hh + 1) * HEAD_DIM, :]
            acc = acc + jnp.dot(vt, pb[c * tk:(c + 1) * tk, :], preferred_element_type=F32)
        acc_ref[hh] = acc
        m_ref[hh] = m_new

    def tile(it, mask):
        s_next = scores(0, it)
        for hh in range(nh):
            s = s_next
            if hh + 1 < nh:
                s_next = scores(hh + 1, it)
            if mask is not None:
                s = jnp.where(mask, s, -jnp.inf)
            update(hh, it, s)

    def body(it, carry):
        tile(it, None)
        return carry

    lax.fori_loop(0, i, body, 0)
    kv_local = lax.broadcasted_iota(jnp.int32, (tq, tq), 0)
    q_local = lax.broadcasted_iota(jnp.int32, (tq, tq), 1)
    tile(i, kv_local <= q_local)
    out = jnp.concatenate([acc_ref[hh] / l_ref[hh] for hh in range(nh)], axis=0)
    o_ref[...] = out.T.astype(BF16)


def _attn_prompt(qa, ka, vtb):
    b, n_heads, t, _ = qa.shape
    n_kv, d, tk = vtb.shape[1], vtb.shape[2], vtb.shape[3]
    tq = ATTN_Q
    nh = ATTN_HEADS
    return pl.pallas_call(
        functools.partial(_attn_prompt_kernel, tq=tq, tk=tk, nh=nh),
        grid=(b, n_heads // nh, t // tq),
        in_specs=[
            pl.BlockSpec((None, nh, tq, LANES), lambda bi, j, i: (bi, j, i, 0)),
            pl.BlockSpec((None, nh, t, LANES), lambda bi, j, i: (bi, j, 0, 0)),
            pl.BlockSpec((None, n_kv, nh * HEAD_DIM, tk), lambda bi, j, i: (bi, 0, j, 0)),
        ],
        out_specs=pl.BlockSpec((None, tq, nh * HEAD_DIM), lambda bi, j, i: (bi, i, j)),
        out_shape=jax.ShapeDtypeStruct((b, t, d), BF16),
        scratch_shapes=[
            pltpu.VMEM((nh, 1, tq), F32),
            pltpu.VMEM((nh, 1, tq), F32),
            pltpu.VMEM((nh, HEAD_DIM, tq), F32),
        ],
        compiler_params=_params("arbitrary", "arbitrary", "arbitrary"),
        name="attn_prompt",
    )(qa, ka, vtb)


def _head_sums(x):
    return jnp.sum(x.reshape(N_HEADS, HEAD_DIM, x.shape[1]), axis=1)


def _attn_sample_kernel(pt_ref, q_ref, kn_ref, vn_ref, lfn_ref, *refs, n_pages, n_chunks):
    del pt_ref
    k_refs = refs[0:n_pages]
    v_refs = refs[n_pages:2 * n_pages]
    lf_refs = refs[2 * n_pages:3 * n_pages]
    o_ref, qb_ref, m_ref, l_ref, acc_ref, fc_ref = refs[3 * n_pages:]
    c = pl.program_id(1)

    @pl.when(c == 0)
    def _():
        qb_ref[...] = _lane_broadcast_column(q_ref[...])
        m_ref[...] = jnp.full_like(m_ref, -jnp.inf)
        l_ref[...] = jnp.zeros_like(l_ref)
        acc_ref[...] = jnp.zeros_like(acc_ref)
        fc_ref[...] = jnp.zeros_like(fc_ref)

    qb = qb_ref[...]
    row = lax.broadcasted_iota(jnp.int32, (PAGE_SIZE, PAGE_SIZE), 0)
    col = lax.broadcasted_iota(jnp.int32, (PAGE_SIZE, PAGE_SIZE), 1)
    upper = (row <= col).astype(BF16)

    f_run = fc_ref[...]
    zs = []
    for r_ in range(n_pages):
        s = _head_sums(k_refs[r_][...] * qb)
        f = f_run
        for piece in _split_bf16(lf_refs[r_][...] * LOG2E):
            f = f + jnp.dot(piece, upper, preferred_element_type=F32)
        f_run = f[:, PAGE_SIZE - 1:PAGE_SIZE]
        zs.append(s - f)
    fc_ref[...] = f_run
    z = jnp.concatenate(zs, axis=1)

    m_old = m_ref[...]
    m_new = jnp.maximum(m_old, jnp.max(z, axis=1, keepdims=True))
    p = jnp.exp2(z - m_new)
    alpha = jnp.exp2(m_old - m_new)
    l_new = alpha * l_ref[...] + jnp.sum(p, axis=1, keepdims=True)
    for hd in range(N_HEADS):
        rows = slice(hd * HEAD_DIM, (hd + 1) * HEAD_DIM)
        a = acc_ref[rows, :] * alpha[hd:hd + 1, :]
        for r_ in range(n_pages):
            a = a + v_refs[r_][rows, :] * p[hd:hd + 1, r_ * PAGE_SIZE:(r_ + 1) * PAGE_SIZE]
        acc_ref[rows, :] = a
    m_ref[...] = m_new
    l_ref[...] = l_new

    @pl.when(c == n_chunks - 1)
    def _():
        s_new = _head_sums(_lane_broadcast_column(kn_ref[...]) * qb)[:, 0:1]
        z_new = s_new - (f_run + lfn_ref[...] * LOG2E)
        m_fin = jnp.maximum(m_new, z_new)
        p_new = jnp.exp2(z_new - m_fin)
        a_fin = jnp.exp2(m_new - m_fin)
        inv_l = 1.0 / (a_fin * l_new + p_new)
        vnb = _lane_broadcast_column(vn_ref[...])
        lane0 = lax.broadcasted_iota(jnp.int32, (HEAD_DIM, LANES), 1) == 0
        for hd in range(N_HEADS):
            rows = slice(hd * HEAD_DIM, (hd + 1) * HEAD_DIM)
            a = acc_ref[rows, :] * a_fin[hd:hd + 1, :]
            a = a + jnp.where(lane0, vnb[rows, :] * p_new[hd:hd + 1, :], 0.0)
            acc_ref[rows, :] = a * inv_l[hd:hd + 1, :]
        o_ref[...] = jnp.sum(acc_ref[...].T, axis=0, keepdims=True).astype(BF16)


def _attn_sample(page_table, q, kn, vn, lfn, cache_kt, cache_vt, cache_lft, layer):
    bs, d = q.shape
    n_log = page_table.shape[1]
    n_pages = DEC_PAGES
    n_chunks = n_log // n_pages

    def page_map(r_):
        return lambda b, c, pt: (layer, pt[b, c * n_pages + r_], 0, 0)

    tok = lambda b, c, pt: (b, 0, 0)
    in_specs = [pl.BlockSpec((None, 1, d), tok)] * 3 + [pl.BlockSpec((None, N_HEADS, 1), tok)]
    in_specs += [pl.BlockSpec((None, None, d, PAGE_SIZE), page_map(r_)) for r_ in range(n_pages)]
    in_specs += [pl.BlockSpec((None, None, d, PAGE_SIZE), page_map(r_)) for r_ in range(n_pages)]
    in_specs += [pl.BlockSpec((None, None, N_HEADS, PAGE_SIZE), page_map(r_)) for r_ in range(n_pages)]
    grid_spec = pltpu.PrefetchScalarGridSpec(
        num_scalar_prefetch=1,
        grid=(bs, n_chunks),
        in_specs=in_specs,
        out_specs=pl.BlockSpec((None, 1, d), tok),
        scratch_shapes=[
            pltpu.VMEM((d, LANES), F32),
            pltpu.VMEM((N_HEADS, 1), F32),
            pltpu.VMEM((N_HEADS, 1), F32),
            pltpu.VMEM((d, LANES), F32),
            pltpu.VMEM((N_HEADS, 1), F32),
        ],
    )
    out = pl.pallas_call(
        functools.partial(_attn_sample_kernel, n_pages=n_pages, n_chunks=n_chunks),
        grid_spec=grid_spec,
        out_shape=jax.ShapeDtypeStruct((bs, 1, d), BF16),
        compiler_params=_params("arbitrary", "arbitrary"),
        name="attn_sample",
    )(page_table, q.reshape(bs, 1, d), kn.reshape(bs, 1, d), vn.reshape(bs, 1, d),
      lfn.reshape(bs, N_HEADS, 1), *([cache_kt] * n_pages), *([cache_vt] * n_pages),
      *([cache_lft] * n_pages))
    return out.reshape(bs, d)


def _feature_major_pages(cache):
    if cache.ndim == 5:
        n_layers, n_phys, page, nh, hd = cache.shape
        return jnp.transpose(cache, (0, 1, 3, 4, 2)).reshape(n_layers, n_phys, nh * hd, page)
    return jnp.transpose(cache, (0, 1, 3, 2))


def kernel(x_prompt, x_sample, cache_k, cache_v, cache_logf, state_pool, page_table, norm_mix, norm_mlp,
           norm_final, pool_w, pool_scale, w_qkv, w_f, b_f, w_o, w_up, w_down):
    b, t, d = x_prompt.shape
    bs, ts, _ = x_sample.shape
    assert ts == 1 and d == D_MODEL
    depth = norm_mix.shape[0]
    past = page_table.shape[1] * PAGE_SIZE

    ckt = _feature_major_pages(cache_k)
    cvt = _feature_major_pages(cache_v)
    clft = _feature_major_pages(cache_logf)
    ones = jnp.ones((d,), F32)

    xp = x_prompt.reshape(b * t, d)
    xs = x_sample.reshape(bs, d)
    kp, vp, lp, pp = [], [], [], []
    ksl, vsl, lsl, psl = [], [], [], []
    for i in range(depth):
        j = i // 2
        if i % 2 == 0:
            a_p, buf = _pool_prompt(xp.reshape(b, t, d), norm_mix[i])
            a_p = a_p.reshape(b * t, d)
            pp.append(buf[:, HALO - POOL_BUF:, :])
            a_s, nst = _pool_sample(xs, jnp.transpose(state_pool[j], (1, 0, 2)), norm_mix[i], past)
            psl.append(jnp.transpose(nst, (1, 0, 2)))
            wm = pool_w[j].astype(BF16)
            sc = pool_scale[j]
        else:
            wqk = jnp.concatenate(
                [w_qkv[j, :, 0:2 * d], jnp.pad(w_f[j], ((0, 0), (0, LANES - N_HEADS)))], axis=1).astype(BF16)
            wvt = w_qkv[j, :, 2 * d:3 * d].T.astype(BF16)
            bfp = jnp.pad(b_f[j], (0, LANES - N_HEADS)).reshape(1, LANES)
            kt, vt, lft, qa, ka, vtb = _fox_proj_prompt(xp.reshape(b, t, d), norm_mix[i], wqk, wvt, bfp)
            a_p = _attn_prompt(qa, ka, vtb).reshape(b * t, d)
            kp.append(jnp.transpose(kt.reshape(b, N_HEADS, HEAD_DIM, t), (0, 3, 1, 2)))
            vp.append(jnp.transpose(vt.reshape(b, N_HEADS, HEAD_DIM, t), (0, 3, 1, 2)))
            lp.append(jnp.transpose(lft, (0, 2, 1)))
            q2, k2, v2, l2 = _fox_proj_sample(xs, norm_mix[i], wqk, wvt, bfp)
            a_s = _attn_sample(page_table, q2, k2, v2, l2, ckt, cvt, clft, j)
            ksl.append(k2.reshape(bs, 1, N_HEADS, HEAD_DIM))
            vsl.append(v2.reshape(bs, 1, N_HEADS, HEAD_DIM))
            lsl.append(l2.reshape(bs, 1, N_HEADS))
            wm = w_o[j].astype(BF16)
            sc = ones
        wu = w_up[i].astype(BF16)
        wd = w_down[i].astype(BF16)
        xp = _mix_mlp(xp, a_p, wm, sc, norm_mlp[i], wu, wd)
        xs = _mix_mlp(xs, a_s, wm, sc, norm_mlp[i], wu, wd)
    y_prompt = _final_norm(xp, norm_final).reshape(b, t, d)
    y_sample = _final_norm(xs, norm_final).reshape(bs, 1, d)
    return (y_prompt, y_sample,
            jnp.stack(kp), jnp.stack(vp), jnp.stack(lp), jnp.stack(pp),
            jnp.stack(ksl), jnp.stack(vsl), jnp.stack(lsl), jnp.stack(psl))
```

```python
import functools

import numpy as np
import jax
import jax.numpy as jnp
from jax import lax
from jax.experimental import pallas as pl
from jax.experimental.pallas import tpu as pltpu

F32 = jnp.float32
BF16 = jnp.bfloat16

D_MODEL = 1024
N_HEADS = 16
HEAD_DIM = 64
D_FF = 4 * D_MODEL
POOL_WINDOWS = (2, 4, 8, 16)
POOL_GROUP = D_MODEL // len(POOL_WINDOWS)
POOL_BUF = max(POOL_WINDOWS) - 1
PAGE_SIZE = 128
RMS_EPS = 1e-6

LANES = 128
HALO = 16
VMEM_LIMIT = 60 * 1024 * 1024

LOG2E = 1.4426950408889634
Q_SCALE = HEAD_DIM ** -0.5 * LOG2E

MLP_ROWS = 512
MLP_FF_CHUNK = 1024
POOL_ROWS = 512
PROJ_ROWS = 256
ATTN_Q = 512
ATTN_HEADS = 4
N_SPLIT = 3


def _rmsnorm(x, g):
    r = lax.rsqrt(jnp.mean(x * x, axis=-1, keepdims=True) + RMS_EPS)
    return (x * r) * g


def _split_bf16(x):
    pieces = []
    rem = x
    for _ in range(N_SPLIT):
        p = rem.astype(BF16)
        pieces.append(p)
        rem = rem - p.astype(F32)
    return pieces


def _log_sigmoid(z):
    return jnp.minimum(z, 0.0) - jnp.log1p(jnp.exp(-jnp.abs(z)))


def _lane_broadcast_column(row):
    return jnp.broadcast_to(row, (LANES, row.shape[1])).T


def _params(*semantics):
    return pltpu.CompilerParams(dimension_semantics=semantics, vmem_limit_bytes=VMEM_LIMIT)


def _resident(shape):
    return pl.BlockSpec(shape, lambda *_: (0,) * len(shape), pipeline_mode=pl.Buffered(1))


def _pool_prompt_kernel(x_ref, halo_ref, g_ref, a_ref, buf_ref, *lvl_refs, tm, n_tiles):
    i = pl.program_id(1)
    g = g_ref[...]
    h = _rmsnorm(x_ref[...], g)
    hh = _rmsnorm(halo_ref[...], g)
    top = 2 * HALO
    end = top + tm
    e0 = lvl_refs[0]
    e0[0:HALO, :] = jnp.zeros((HALO, h.shape[1]), F32)
    e0[HALO:top, :] = jnp.where(i == 0, 0.0, hh)
    e0[top:end, :] = h
    pos = i * tm + lax.broadcasted_iota(jnp.int32, (tm, 1), 0)
    outs = []
    prev = e0
    for lvl, w in enumerate(POOL_WINDOWS):
        first = 8 * (lvl + 1)
        shift = w // 2
        drop = POOL_GROUP if lvl else 0
        if lvl + 1 < len(POOL_WINDOWS):
            cur = lvl_refs[lvl + 1]
            cur[first:end, :] = prev[first:end, drop:] + prev[first - shift:end - shift, drop:]
            s = cur[top:end, 0:POOL_GROUP]
            prev = cur
        else:
            s = prev[top:end, drop:] + prev[top - shift:end - shift, drop:]
        cnt = jnp.minimum(w, pos + 1).astype(F32)
        outs.append(s / cnt)
    pooled = jnp.concatenate(outs, axis=-1)
    a_ref[...] = (pooled - h).astype(BF16)

    @pl.when(i == n_tiles - 1)
    def _():
        buf_ref[...] = e0[end - HALO:end, :]


def _pool_prompt(x, g):
    b, t, d = x.shape
    tm = POOL_ROWS
    n_tiles = t // tm
    per = tm // HALO
    return pl.pallas_call(
        functools.partial(_pool_prompt_kernel, tm=tm, n_tiles=n_tiles),
        grid=(b, n_tiles),
        in_specs=[
            pl.BlockSpec((None, tm, d), lambda bi, i: (bi, i, 0)),
            pl.BlockSpec((None, HALO, d), lambda bi, i: (bi, jnp.maximum(i * per - 1, 0), 0)),
            pl.BlockSpec((1, d), lambda bi, i: (0, 0)),
        ],
        out_specs=[
            pl.BlockSpec((None, tm, d), lambda bi, i: (bi, i, 0)),
            pl.BlockSpec((None, HALO, d), lambda bi, i: (bi, 0, 0)),
        ],
        out_shape=[
            jax.ShapeDtypeStruct((b, t, d), BF16),
            jax.ShapeDtypeStruct((b, HALO, d), F32),
        ],
        scratch_shapes=[pltpu.VMEM((2 * HALO + tm, d - max(lvl - 1, 0) * POOL_GROUP), F32)
                        for lvl in range(len(POOL_WINDOWS))],
        compiler_params=_params("arbitrary", "arbitrary"),
        name="pool_prompt",
    )(x, x, g.reshape(1, d))


def _pool_sample_kernel(x_ref, st_ref, g_ref, a_ref, nst_ref, *, pos):
    h = _rmsnorm(x_ref[...], g_ref[...])
    outs = []
    for gi, w in enumerate(POOL_WINDOWS):
        lo, hi = gi * POOL_GROUP, (gi + 1) * POOL_GROUP
        s = h[:, lo:hi]
        for j in range(1, w):
            s = s + st_ref[POOL_BUF - j, :, lo:hi]
        outs.append(s / float(min(w, pos + 1)))
    pooled = jnp.concatenate(outs, axis=-1)
    a_ref[...] = (pooled - h).astype(BF16)
    for r in range(POOL_BUF - 1):
        nst_ref[r] = st_ref[r + 1]
    nst_ref[POOL_BUF - 1] = h


def _pool_sample(x, state_t, g, pos):
    bs, d = x.shape
    return pl.pallas_call(
        functools.partial(_pool_sample_kernel, pos=pos),
        out_shape=[
            jax.ShapeDtypeStruct((bs, d), BF16),
            jax.ShapeDtypeStruct((POOL_BUF, bs, d), F32),
        ],
        compiler_params=pltpu.CompilerParams(vmem_limit_bytes=VMEM_LIMIT),
        name="pool_sample",
    )(x, state_t, g.reshape(1, d))


def _mix_mlp_kernel(x_ref, a_ref, wm_ref, sc_ref, g_ref, wu_ref, wd_ref, o_ref):
    if len(wm_ref.shape) == 3:
        gw = wm_ref.shape[1]
        y = jnp.concatenate(
            [jnp.dot(a_ref[:, gi * gw:(gi + 1) * gw], wm_ref[gi], preferred_element_type=F32)
             for gi in range(wm_ref.shape[0])], axis=-1)
    else:
        y = jnp.dot(a_ref[...], wm_ref[...], preferred_element_type=F32)
    x1 = x_ref[...] + y * sc_ref[...]
    h = _rmsnorm(x1, g_ref[...]).astype(BF16)
    acc = x1
    for c in range(D_FF // MLP_FF_CHUNK):
        cols = slice(c * MLP_FF_CHUNK, (c + 1) * MLP_FF_CHUNK)
        u = jnp.maximum(jnp.dot(h, wu_ref[:, cols], preferred_element_type=F32), 0.0)
        acc = acc + jnp.dot((u * u).astype(BF16), wd_ref[cols, :], preferred_element_type=F32)
    o_ref[...] = acc


def _mix_mlp(x, a, wm, sc, g, wu, wd):
    n, d = x.shape
    tm = min(MLP_ROWS, n)
    rows = lambda i: (i, 0)
    return pl.pallas_call(
        _mix_mlp_kernel,
        grid=(n // tm,),
        in_specs=[
            pl.BlockSpec((tm, d), rows),
            pl.BlockSpec((tm, d), rows),
            _resident(wm.shape),
            _resident((1, d)),
            _resident((1, d)),
            _resident((d, D_FF)),
            _resident((D_FF, d)),
        ],
        out_specs=pl.BlockSpec((tm, d), rows),
        out_shape=jax.ShapeDtypeStruct((n, d), F32),
        compiler_params=_params("arbitrary"),
        name="mix_mlp",
    )(x, a, wm, sc.reshape(1, d), g.reshape(1, d), wu, wd)


def _final_norm_kernel(x_ref, g_ref, o_ref):
    o_ref[...] = _rmsnorm(x_ref[...], g_ref[...])


def _final_norm(x, g):
    n, d = x.shape
    tm = min(MLP_ROWS, n)
    return pl.pallas_call(
        _final_norm_kernel,
        grid=(n // tm,),
        in_specs=[pl.BlockSpec((tm, d), lambda i: (i, 0)), pl.BlockSpec((1, d), lambda i: (0, 0))],
        out_specs=pl.BlockSpec((tm, d), lambda i: (i, 0)),
        out_shape=jax.ShapeDtypeStruct((n, d), F32),
        compiler_params=_params("arbitrary"),
        name="final_norm",
    )(x, g.reshape(1, d))


def _aug_constants():
    width = (N_HEADS // 2) * LANES
    pq = np.zeros((LANES, width), np.float32)
    pk = np.zeros((LANES, width), np.float32)
    oq = np.zeros((1, width), np.float32)
    ok = np.zeros((1, width), np.float32)
    for h in range(N_HEADS):
        base = (h // 2) * LANES + (HEAD_DIM if h % 2 == 0 else 0)
        for p in range(N_SPLIT):
            oq[0, base + p] = 1.0
            pq[p * N_HEADS + h, base + N_SPLIT + p] = 1.0
            pk[p * N_HEADS + h, base + p] = 1.0
            ok[0, base + N_SPLIT + p] = 1.0
    return (jnp.asarray(pq, BF16), jnp.asarray(pk, BF16), jnp.asarray(oq), jnp.asarray(ok))


def _gate_logits(z, bf_ref):
    z = z + bf_ref[...]
    lane = lax.broadcasted_iota(jnp.int32, z.shape, 1)
    return jnp.where(lane < N_HEADS, _log_sigmoid(z), 0.0)


def _fox_proj_prompt_kernel(x_ref, g_ref, w_ref, wvt_ref, bf_ref, pq_ref, pk_ref, oq_ref, ok_ref,
                            kt_ref, vt_ref, lft_ref, qa_ref, ka_ref, vtb_ref, carry_ref, *, tm):
    i = pl.program_id(1)

    @pl.when(i == 0)
    def _():
        carry_ref[...] = jnp.zeros_like(carry_ref)

    h = _rmsnorm(x_ref[...], g_ref[...]).astype(BF16)
    qk = jnp.dot(h, w_ref[...], preferred_element_type=F32)
    q = qk[:, 0:D_MODEL] * Q_SCALE
    k = qk[:, D_MODEL:2 * D_MODEL]
    lf = _gate_logits(qk[:, 2 * D_MODEL:2 * D_MODEL + LANES], bf_ref)
    vt = lax.dot_general(wvt_ref[...], h, (((1,), (1,)), ((), ())), preferred_element_type=F32)
    kt_ref[...] = k.T
    vt_ref[...] = vt
    vtb_ref[...] = vt.astype(BF16)
    lft_ref[...] = lf.T[0:N_HEADS, :]

    row = lax.broadcasted_iota(jnp.int32, (tm, tm), 0)
    col = lax.broadcasted_iota(jnp.int32, (tm, tm), 1)
    tri = (row >= col).astype(BF16)
    f = carry_ref[...]
    for piece in _split_bf16(lf * LOG2E):
        f = f + jnp.dot(tri, piece, preferred_element_type=F32)
    carry_ref[...] = f[tm - 1:tm, :]

    packed = None
    for p, piece in enumerate(_split_bf16(f)):
        shifted = piece.astype(F32)
        if p:
            shifted = pltpu.roll(shifted, p * N_HEADS, axis=1)
        packed = shifted if packed is None else packed + shifted
    packed = packed.astype(BF16)
    aug_q = jnp.dot(packed, pq_ref[...], preferred_element_type=F32) + oq_ref[...]
    aug_k = jnp.dot(-packed, pk_ref[...], preferred_element_type=F32) + ok_ref[...]

    lane = lax.broadcasted_iota(jnp.int32, (tm, LANES), 1)
    for hd in range(N_HEADS):
        slab = slice((hd // 2) * LANES, (hd // 2 + 1) * LANES)
        keep = (lane < HEAD_DIM) if hd % 2 == 0 else (lane >= HEAD_DIM)
        qa_ref[hd] = jnp.where(keep, q[:, slab], aug_q[:, slab]).astype(BF16)
        ka_ref[hd] = jnp.where(keep, k[:, slab], aug_k[:, slab]).astype(BF16)


def _fox_proj_prompt(x, g, wqk, wvt, bfp):
    b, t, d = x.shape
    tm = PROJ_ROWS
    n_tiles = t // tm
    pq, pk, oq, ok = _aug_constants()
    rows = lambda bi, i: (bi, i, 0)
    cols = lambda bi, i: (bi, 0, i)
    return pl.pallas_call(
        functools.partial(_fox_proj_prompt_kernel, tm=tm),
        grid=(b, n_tiles),
        in_specs=[
            pl.BlockSpec((None, tm, d), rows),
            _resident((1, d)),
            _resident(wqk.shape),
            _resident(wvt.shape),
            _resident((1, LANES)),
            _resident(pq.shape),
            _resident(pk.shape),
            _resident(oq.shape),
            _resident(ok.shape),
        ],
        out_specs=[
            pl.BlockSpec((None, d, tm), cols),
            pl.BlockSpec((None, d, tm), cols),
            pl.BlockSpec((None, N_HEADS, tm), cols),
            pl.BlockSpec((None, N_HEADS, tm, LANES), lambda bi, i: (bi, 0, i, 0)),
            pl.BlockSpec((None, N_HEADS, tm, LANES), lambda bi, i: (bi, 0, i, 0)),
            pl.BlockSpec((None, None, d, tm), lambda bi, i: (bi, i, 0, 0)),
        ],
        out_shape=[
            jax.ShapeDtypeStruct((b, d, t), F32),
            jax.ShapeDtypeStruct((b, d, t), F32),
            jax.ShapeDtypeStruct((b, N_HEADS, t), F32),
            jax.ShapeDtypeStruct((b, N_HEADS, t, LANES), BF16),
            jax.ShapeDtypeStruct((b, N_HEADS, t, LANES), BF16),
            jax.ShapeDtypeStruct((b, n_tiles, d, tm), BF16),
        ],
        scratch_shapes=[pltpu.VMEM((1, LANES), F32)],
        compiler_params=_params("arbitrary", "arbitrary"),
        name="fox_proj_prompt",
    )(x, g.reshape(1, d), wqk, wvt, bfp, pq, pk, oq, ok)


def _fox_proj_sample_kernel(x_ref, g_ref, w_ref, wvt_ref, bf_ref, q_ref, k_ref, v_ref, lf_ref):
    h = _rmsnorm(x_ref[...], g_ref[...]).astype(BF16)
    qk = jnp.dot(h, w_ref[...], preferred_element_type=F32)
    q_ref[...] = qk[:, 0:D_MODEL] * Q_SCALE
    k_ref[...] = qk[:, D_MODEL:2 * D_MODEL]
    v_ref[...] = lax.dot_general(h, wvt_ref[...], (((1,), (1,)), ((), ())), preferred_element_type=F32)
    lf_ref[...] = _gate_logits(qk[:, 2 * D_MODEL:2 * D_MODEL + LANES], bf_ref)[:, 0:N_HEADS]


def _fox_proj_sample(x, g, wqk, wvt, bfp):
    bs, d = x.shape
    return pl.pallas_call(
        _fox_proj_sample_kernel,
        out_shape=[
            jax.ShapeDtypeStruct((bs, d), F32),
            jax.ShapeDtypeStruct((bs, d), F32),
            jax.ShapeDtypeStruct((bs, d), F32),
            jax.ShapeDtypeStruct((bs, N_HEADS), F32),
        ],
        compiler_params=pltpu.CompilerParams(vmem_limit_bytes=VMEM_LIMIT),
        name="fox_proj_sample",
    )(x, g.reshape(1, d), wqk, wvt, bfp)


def _prompt_tile_attention(i, q_ref, k_ref, vt_ref, o_ref, m_ref, l_ref, acc_ref, *, tq, tk, nh):
    n_sub = tq // tk
    m_ref[...] = jnp.full_like(m_ref, -jnp.inf)
    l_ref[...] = jnp.zeros_like(l_ref)
    acc_ref[...] = jnp.zeros_like(acc_ref)

    def scores(hh, it):
        kb = k_ref[hh, pl.ds(pl.multiple_of(it * tq, tq), tq), :]
        return lax.dot_general(kb, q_ref[hh], (((1,), (1,)), ((), ())), preferred_element_type=F32)

    def update(hh, it, s):
        m_old = m_ref[hh]
        m_new = jnp.maximum(m_old, jnp.max(s, axis=0, keepdims=True))
        p = jnp.exp2(s - m_new)
        alpha = jnp.exp2(m_old - m_new)
        l_ref[hh] = alpha * l_ref[hh] + jnp.sum(p, axis=0, keepdims=True)
        pb = p.astype(BF16)
        acc = alpha * acc_ref[hh]
        for c in range(n_sub):
            vt = vt_ref[it * n_sub + c, hh * HEAD_DIM:(hh + 1) * HEAD_DIM, :]
            acc = acc + jnp.dot(vt, pb[c * tk:(c + 1) * tk, :], preferred_element_type=F32)
        acc_ref[hh] = acc
        m_ref[hh] = m_new

    def tile(it, mask):
        s_next = scores(0, it)
        for hh in range(nh):
            s = s_next
            if hh + 1 < nh:
                s_next = scores(hh + 1, it)
            if mask is not None:
                s = jnp.where(mask, s, -jnp.inf)
            update(hh, it, s)

    def body(it, carry):
        tile(it, None)
        return carry

    lax.fori_loop(0, i, body, 0)
    kv_local = lax.broadcasted_iota(jnp.int32, (tq, tq), 0)
    q_local = lax.broadcasted_iota(jnp.int32, (tq, tq), 1)
    tile(i, kv_local <= q_local)
    out = jnp.concatenate([acc_ref[hh] / l_ref[hh] for hh in range(nh)], axis=0)
    o_ref[...] = out.T.astype(BF16)


def _head_sums(x):
    return jnp.sum(x.reshape(N_HEADS, HEAD_DIM, x.shape[1]), axis=1)


def _sample_chunk_attention(c, q_ref, kn_ref, vn_ref, lfn_ref, k_refs, v_refs, lf_refs,
                            o_ref, qb_ref, m_ref, l_ref, acc_ref, fc_ref, *, n_chunks):
    n_pages = len(k_refs)

    @pl.when(c == 0)
    def _():
        qb_ref[...] = _lane_broadcast_column(q_ref[...])
        m_ref[...] = jnp.full_like(m_ref, -jnp.inf)
        l_ref[...] = jnp.zeros_like(l_ref)
        acc_ref[...] = jnp.zeros_like(acc_ref)
        fc_ref[...] = jnp.zeros_like(fc_ref)

    qb = qb_ref[...]
    row = lax.broadcasted_iota(jnp.int32, (PAGE_SIZE, PAGE_SIZE), 0)
    col = lax.broadcasted_iota(jnp.int32, (PAGE_SIZE, PAGE_SIZE), 1)
    upper = (row <= col).astype(BF16)

    f_run = fc_ref[...]
    zs = []
    for r_ in range(n_pages):
        s = _head_sums(k_refs[r_][...] * qb)
        f = f_run
        for piece in _split_bf16(lf_refs[r_][...] * LOG2E):
            f = f + jnp.dot(piece, upper, preferred_element_type=F32)
        f_run = f[:, PAGE_SIZE - 1:PAGE_SIZE]
        zs.append(s - f)
    fc_ref[...] = f_run
    z = jnp.concatenate(zs, axis=1)

    m_old = m_ref[...]
    m_new = jnp.maximum(m_old, jnp.max(z, axis=1, keepdims=True))
    p = jnp.exp2(z - m_new)
    alpha = jnp.exp2(m_old - m_new)
    l_new = alpha * l_ref[...] + jnp.sum(p, axis=1, keepdims=True)
    for hd in range(N_HEADS):
        rows = slice(hd * HEAD_DIM, (hd + 1) * HEAD_DIM)
        a = acc_ref[rows, :] * alpha[hd:hd + 1, :]
        for r_ in range(n_pages):
            a = a + v_refs[r_][rows, :] * p[hd:hd + 1, r_ * PAGE_SIZE:(r_ + 1) * PAGE_SIZE]
        acc_ref[rows, :] = a
    m_ref[...] = m_new
    l_ref[...] = l_new

    @pl.when(c == n_chunks - 1)
    def _():
        s_new = _head_sums(_lane_broadcast_column(kn_ref[...]) * qb)[:, 0:1]
        z_new = s_new - (f_run + lfn_ref[...] * LOG2E)
        m_fin = jnp.maximum(m_new, z_new)
        p_new = jnp.exp2(z_new - m_fin)
        a_fin = jnp.exp2(m_new - m_fin)
        inv_l = 1.0 / (a_fin * l_new + p_new)
        vnb = _lane_broadcast_column(vn_ref[...])
        lane0 = lax.broadcasted_iota(jnp.int32, (HEAD_DIM, LANES), 1) == 0
        for hd in range(N_HEADS):
            rows = slice(hd * HEAD_DIM, (hd + 1) * HEAD_DIM)
            a = acc_ref[rows, :] * a_fin[hd:hd + 1, :]
            a = a + jnp.where(lane0, vnb[rows, :] * p_new[hd:hd + 1, :], 0.0)
            acc_ref[rows, :] = a * inv_l[hd:hd + 1, :]
        o_ref[...] = jnp.sum(acc_ref[...].T, axis=0, keepdims=True).astype(BF16)


def _attn_kernel(pt_ref, q_ref, k_ref, vt_ref, qs_ref, kn_ref, vn_ref, lfn_ref, *refs,
                 tq, tk, nh, n_pages, n_chunks):
    del pt_ref
    k_refs = refs[0:n_pages]
    v_refs = refs[n_pages:2 * n_pages]
    lf_refs = refs[2 * n_pages:3 * n_pages]
    o_ref, os_ref, pm_ref, pl_ref, pacc_ref, qb_ref, sm_ref, sl_ref, sacc_ref, fc_ref = refs[3 * n_pages:]
    i = pl.program_id(2)
    _sample_chunk_attention(lax.rem(i, n_chunks), qs_ref, kn_ref, vn_ref, lfn_ref, k_refs, v_refs, lf_refs,
                            os_ref, qb_ref, sm_ref, sl_ref, sacc_ref, fc_ref, n_chunks=n_chunks)
    _prompt_tile_attention(i, q_ref, k_ref, vt_ref, o_ref, pm_ref, pl_ref, pacc_ref, tq=tq, tk=tk, nh=nh)


def _attn(qa, ka, vtb, page_table, q, kn, vn, lfn, cache_kt, cache_vt, cache_lft, layer):
    b, n_heads, t, _ = qa.shape
    n_kv, d, tk = vtb.shape[1], vtb.shape[2], vtb.shape[3]
    tq = ATTN_Q
    nh = ATTN_HEADS
    ng, nq = n_heads // nh, t // tq
    bs = q.shape[0]
    n_log = page_table.shape[1]
    n_chunks, rem = divmod(b * ng * nq, bs)
    assert rem == 0 and nq % n_chunks == 0 and n_log % n_chunks == 0
    n_pages = n_log // n_chunks
    seqs_per_row = nq // n_chunks

    def seq(bi, j, i):
        return (bi * ng + j) * seqs_per_row + i // n_chunks

    def page_map(r_):
        return lambda bi, j, i, pt: (layer, pt[seq(bi, j, i), lax.rem(i, n_chunks) * n_pages + r_], 0, 0)

    tok = lambda bi, j, i, pt: (seq(bi, j, i), 0, 0)
    in_specs = [
        pl.BlockSpec((None, nh, tq, LANES), lambda bi, j, i, pt: (bi, j, i, 0)),
        pl.BlockSpec((None, nh, t, LANES), lambda bi, j, i, pt: (bi, j, 0, 0)),
        pl.BlockSpec((None, n_kv, nh * HEAD_DIM, tk), lambda bi, j, i, pt: (bi, 0, j, 0)),
    ]
    in_specs += [pl.BlockSpec((None, 1, d), tok)] * 3 + [pl.BlockSpec((None, N_HEADS, 1), tok)]
    in_specs += [pl.BlockSpec((None, None, d, PAGE_SIZE), page_map(r_)) for r_ in range(n_pages)]
    in_specs += [pl.BlockSpec((None, None, d, PAGE_SIZE), page_map(r_)) for r_ in range(n_pages)]
    in_specs += [pl.BlockSpec((None, None, N_HEADS, PAGE_SIZE), page_map(r_)) for r_ in range(n_pages)]
    grid_spec = pltpu.PrefetchScalarGridSpec(
        num_scalar_prefetch=1,
        grid=(b, ng, nq),
        in_specs=in_specs,
        out_specs=[
            pl.BlockSpec((None, tq, nh * HEAD_DIM), lambda bi, j, i, pt: (bi, i, j)),
            pl.BlockSpec((None, 1, d), tok),
        ],
        scratch_shapes=[
            pltpu.VMEM((nh, 1, tq), F32),
            pltpu.VMEM((nh, 1, tq), F32),
            pltpu.VMEM((nh, HEAD_DIM, tq), F32),
            pltpu.VMEM((d, LANES), F32),
            pltpu.VMEM((N_HEADS, 1), F32),
            pltpu.VMEM((N_HEADS, 1), F32),
            pltpu.VMEM((d, LANES), F32),
            pltpu.VMEM((N_HEADS, 1), F32),
        ],
    )
    o, o_s = pl.pallas_call(
        functools.partial(_attn_kernel, tq=tq, tk=tk, nh=nh, n_pages=n_pages, n_chunks=n_chunks),
        grid_spec=grid_spec,
        out_shape=[jax.ShapeDtypeStruct((b, t, d), BF16), jax.ShapeDtypeStruct((bs, 1, d), BF16)],
        compiler_params=_params("arbitrary", "arbitrary", "arbitrary"),
        name="attn",
    )(page_table, qa, ka, vtb, q.reshape(bs, 1, d), kn.reshape(bs, 1, d), vn.reshape(bs, 1, d),
      lfn.reshape(bs, N_HEADS, 1), *([cache_kt] * n_pages), *([cache_vt] * n_pages),
      *([cache_lft] * n_pages))
    return o, o_s.reshape(bs, d)


def _feature_major_pages(cache):
    if cache.ndim == 5:
        n_layers, n_phys, page, nh, hd = cache.shape
        return jnp.transpose(cache, (0, 1, 3, 4, 2)).reshape(n_layers, n_phys, nh * hd, page)
    return jnp.transpose(cache, (0, 1, 3, 2))


def kernel(x_prompt, x_sample, cache_k, cache_v, cache_logf, state_pool, page_table, norm_mix, norm_mlp,
           norm_final, pool_w, pool_scale, w_qkv, w_f, b_f, w_o, w_up, w_down):
    b, t, d = x_prompt.shape
    bs, ts, _ = x_sample.shape
    assert ts == 1 and d == D_MODEL
    depth = norm_mix.shape[0]
    past = page_table.shape[1] * PAGE_SIZE

    ckt = _feature_major_pages(cache_k)
    cvt = _feature_major_pages(cache_v)
    clft = _feature_major_pages(cache_logf)
    ones = jnp.ones((d,), F32)

    xp = x_prompt.reshape(b * t, d)
    xs = x_sample.reshape(bs, d)
    kp, vp, lp, pp = [], [], [], []
    ksl, vsl, lsl, psl = [], [], [], []
    for i in range(depth):
        j = i // 2
        if i % 2 == 0:
            a_p, buf = _pool_prompt(xp.reshape(b, t, d), norm_mix[i])
            a_p = a_p.reshape(b * t, d)
            pp.append(buf[:, HALO - POOL_BUF:, :])
            a_s, nst = _pool_sample(xs, jnp.transpose(state_pool[j], (1, 0, 2)), norm_mix[i], past)
            psl.append(jnp.transpose(nst, (1, 0, 2)))
            wm = pool_w[j].astype(BF16)
            sc = pool_scale[j]
        else:
            wqk = jnp.concatenate(
                [w_qkv[j, :, 0:2 * d], jnp.pad(w_f[j], ((0, 0), (0, LANES - N_HEADS)))], axis=1).astype(BF16)
            wvt = w_qkv[j, :, 2 * d:3 * d].T.astype(BF16)
            bfp = jnp.pad(b_f[j], (0, LANES - N_HEADS)).reshape(1, LANES)
            kt, vt, lft, qa, ka, vtb = _fox_proj_prompt(xp.reshape(b, t, d), norm_mix[i], wqk, wvt, bfp)
            kp.append(jnp.transpose(kt.reshape(b, N_HEADS, HEAD_DIM, t), (0, 3, 1, 2)))
            vp.append(jnp.transpose(vt.reshape(b, N_HEADS, HEAD_DIM, t), (0, 3, 1, 2)))
            lp.append(jnp.transpose(lft, (0, 2, 1)))
            q2, k2, v2, l2 = _fox_proj_sample(xs, norm_mix[i], wqk, wvt, bfp)
            a_p, a_s = _attn(qa, ka, vtb, page_table, q2, k2, v2, l2, ckt, cvt, clft, j)
            a_p = a_p.reshape(b * t, d)
            ksl.append(k2.reshape(bs, 1, N_HEADS, HEAD_DIM))
            vsl.append(v2.reshape(bs, 1, N_HEADS, HEAD_DIM))
            lsl.append(l2.reshape(bs, 1, N_HEADS))
            wm = w_o[j].astype(BF16)
            sc = ones
        wu = w_up[i].astype(BF16)
        wd = w_down[i].astype(BF16)
        xp = _mix_mlp(xp, a_p, wm, sc, norm_mlp[i], wu, wd)
        xs = _mix_mlp(xs, a_s, wm, sc, norm_mlp[i], wu, wd)
    y_prompt = _final_norm(xp, norm_final).reshape(b, t, d)
    y_sample = _final_norm(xs, norm_final).reshape(bs, 1, d)
    return (y_prompt, y_sample,
            jnp.stack(kp), jnp.stack(vp), jnp.stack(lp), jnp.stack(pp),
            jnp.stack(ksl), jnp.stack(vsl), jnp.stack(lsl), jnp.stack(psl))
```

```python
import functools

import numpy as np
import jax
import jax.numpy as jnp
from jax import lax
from jax.experimental import pallas as pl
from jax.experimental.pallas import tpu as pltpu

F32 = jnp.float32
BF16 = jnp.bfloat16

D_MODEL = 1024
N_HEADS = 16
HEAD_DIM = 64
D_FF = 4 * D_MODEL
POOL_WINDOWS = (2, 4, 8, 16)
POOL_GROUP = D_MODEL // len(POOL_WINDOWS)
POOL_BUF = max(POOL_WINDOWS) - 1
PAGE_SIZE = 128
RMS_EPS = 1e-6

LANES = 128
HALO = 16
VMEM_LIMIT = 60 * 1024 * 1024

LOG2E = 1.4426950408889634
Q_SCALE = HEAD_DIM ** -0.5 * LOG2E

MLP_ROWS = 512
MLP_FF_CHUNK = 1024
POOL_ROWS = 512
PROJ_ROWS = 256
ATTN_Q = 512
ATTN_HEADS = 4
SUM_ROWS = 16
N_SPLIT = 3


def _rmsnorm(x, g):
    r = lax.rsqrt(jnp.mean(x * x, axis=-1, keepdims=True) + RMS_EPS)
    return (x * r) * g


def _split_bf16(x):
    pieces = []
    rem = x
    for _ in range(N_SPLIT):
        p = rem.astype(BF16)
        pieces.append(p)
        rem = rem - p.astype(F32)
    return pieces


def _log_sigmoid(z):
    return jnp.minimum(z, 0.0) - jnp.log1p(jnp.exp(-jnp.abs(z)))


def _lane_broadcast_column(row):
    return jnp.broadcast_to(row, (LANES, row.shape[1])).T


def _params(*semantics):
    return pltpu.CompilerParams(dimension_semantics=semantics, vmem_limit_bytes=VMEM_LIMIT)


def _resident(shape):
    return pl.BlockSpec(shape, lambda *_: (0,) * len(shape), pipeline_mode=pl.Buffered(1))


def _pool_prompt_kernel(x_ref, halo_ref, g_ref, a_ref, buf_ref, *lvl_refs, tm, n_tiles):
    i = pl.program_id(1)
    g = g_ref[...]
    h = _rmsnorm(x_ref[...], g)
    hh = _rmsnorm(halo_ref[...], g)
    top = 2 * HALO
    end = top + tm
    e0 = lvl_refs[0]
    e0[0:HALO, :] = jnp.zeros((HALO, h.shape[1]), F32)
    e0[HALO:top, :] = jnp.where(i == 0, 0.0, hh)
    e0[top:end, :] = h
    pos = i * tm + lax.broadcasted_iota(jnp.int32, (tm, 1), 0)
    outs = []
    prev = e0
    for lvl, w in enumerate(POOL_WINDOWS):
        first = 8 * (lvl + 1)
        shift = w // 2
        drop = POOL_GROUP if lvl else 0
        if lvl + 1 < len(POOL_WINDOWS):
            cur = lvl_refs[lvl + 1]
            cur[first:end, :] = prev[first:end, drop:] + prev[first - shift:end - shift, drop:]
            s = cur[top:end, 0:POOL_GROUP]
            prev = cur
        else:
            s = prev[top:end, drop:] + prev[top - shift:end - shift, drop:]
        cnt = jnp.minimum(w, pos + 1).astype(F32)
        outs.append(s / cnt)
    pooled = jnp.concatenate(outs, axis=-1)
    a_ref[...] = (pooled - h).astype(BF16)

    @pl.when(i == n_tiles - 1)
    def _():
        buf_ref[...] = e0[end - HALO:end, :]


def _pool_prompt(x, g):
    b, t, d = x.shape
    tm = POOL_ROWS
    n_tiles = t // tm
    per = tm // HALO
    return pl.pallas_call(
        functools.partial(_pool_prompt_kernel, tm=tm, n_tiles=n_tiles),
        grid=(b, n_tiles),
        in_specs=[
            pl.BlockSpec((None, tm, d), lambda bi, i: (bi, i, 0)),
            pl.BlockSpec((None, HALO, d), lambda bi, i: (bi, jnp.maximum(i * per - 1, 0), 0)),
            pl.BlockSpec((1, d), lambda bi, i: (0, 0)),
        ],
        out_specs=[
            pl.BlockSpec((None, tm, d), lambda bi, i: (bi, i, 0)),
            pl.BlockSpec((None, HALO, d), lambda bi, i: (bi, 0, 0)),
        ],
        out_shape=[
            jax.ShapeDtypeStruct((b, t, d), BF16),
            jax.ShapeDtypeStruct((b, HALO, d), F32),
        ],
        scratch_shapes=[pltpu.VMEM((2 * HALO + tm, d - max(lvl - 1, 0) * POOL_GROUP), F32)
                        for lvl in range(len(POOL_WINDOWS))],
        compiler_params=_params("arbitrary", "arbitrary"),
        name="pool_prompt",
    )(x, x, g.reshape(1, d))


def _pool_sample_kernel(x_ref, st_ref, g_ref, a_ref, nst_ref, *, pos):
    h = _rmsnorm(x_ref[...], g_ref[...])
    outs = []
    for gi, w in enumerate(POOL_WINDOWS):
        lo, hi = gi * POOL_GROUP, (gi + 1) * POOL_GROUP
        s = h[:, lo:hi]
        for j in range(1, w):
            s = s + st_ref[POOL_BUF - j, :, lo:hi]
        outs.append(s / float(min(w, pos + 1)))
    pooled = jnp.concatenate(outs, axis=-1)
    a_ref[...] = (pooled - h).astype(BF16)
    for r in range(POOL_BUF - 1):
        nst_ref[r] = st_ref[r + 1]
    nst_ref[POOL_BUF - 1] = h


def _pool_sample(x, state_t, g, pos):
    bs, d = x.shape
    return pl.pallas_call(
        functools.partial(_pool_sample_kernel, pos=pos),
        out_shape=[
            jax.ShapeDtypeStruct((bs, d), BF16),
            jax.ShapeDtypeStruct((POOL_BUF, bs, d), F32),
        ],
        compiler_params=pltpu.CompilerParams(vmem_limit_bytes=VMEM_LIMIT),
        name="pool_sample",
    )(x, state_t, g.reshape(1, d))


def _mix_mlp_kernel(x_ref, a_ref, wm_ref, sc_ref, g_ref, wu_ref, wd_ref, *rest):
    o_ref = rest[-1]
    if len(wm_ref.shape) == 3:
        gw = wm_ref.shape[1]
        y = jnp.concatenate(
            [jnp.dot(a_ref[:, gi * gw:(gi + 1) * gw], wm_ref[gi], preferred_element_type=F32)
             for gi in range(wm_ref.shape[0])], axis=-1)
    else:
        y = jnp.dot(a_ref[...], wm_ref[...], preferred_element_type=F32)
    x1 = x_ref[...] + y * sc_ref[...]
    h = _rmsnorm(x1, g_ref[...]).astype(BF16)
    acc = x1
    for c in range(D_FF // MLP_FF_CHUNK):
        cols = slice(c * MLP_FF_CHUNK, (c + 1) * MLP_FF_CHUNK)
        u = jnp.maximum(jnp.dot(h, wu_ref[:, cols], preferred_element_type=F32), 0.0)
        acc = acc + jnp.dot((u * u).astype(BF16), wd_ref[cols, :], preferred_element_type=F32)
    o_ref[...] = _rmsnorm(acc, rest[0][...]) if len(rest) == 2 else acc


def _mix_mlp(x, a, wm, sc, g, wu, wd, g_out=None):
    n, d = x.shape
    tm = min(MLP_ROWS, n)
    rows = lambda i: (i, 0)
    gains = [g.reshape(1, d)] + ([] if g_out is None else [g_out.reshape(1, d)])
    return pl.pallas_call(
        _mix_mlp_kernel,
        grid=(n // tm,),
        in_specs=[
            pl.BlockSpec((tm, d), rows),
            pl.BlockSpec((tm, d), rows),
            _resident(wm.shape),
            _resident((1, d)),
            _resident((1, d)),
            _resident((d, D_FF)),
            _resident((D_FF, d)),
        ] + [_resident((1, d))] * (len(gains) - 1),
        out_specs=pl.BlockSpec((tm, d), rows),
        out_shape=jax.ShapeDtypeStruct((n, d), F32),
        compiler_params=_params("arbitrary"),
        name="mix_mlp",
    )(x, a, wm, sc.reshape(1, d), gains[0], wu, wd, *gains[1:])


def _aug_constants():
    width = (N_HEADS // 2) * LANES
    pq = np.zeros((LANES, width), np.float32)
    pk = np.zeros((LANES, width), np.float32)
    oq = np.zeros((1, width), np.float32)
    ok = np.zeros((1, width), np.float32)
    for h in range(N_HEADS):
        base = (h // 2) * LANES + (HEAD_DIM if h % 2 == 0 else 0)
        for p in range(N_SPLIT):
            oq[0, base + p] = 1.0
            pq[p * N_HEADS + h, base + N_SPLIT + p] = 1.0
            pk[p * N_HEADS + h, base + p] = 1.0
            ok[0, base + N_SPLIT + p] = 1.0
    return (jnp.asarray(pq, BF16), jnp.asarray(pk, BF16), jnp.asarray(oq), jnp.asarray(ok))


def _gate_logits(z, bf_ref):
    z = z + bf_ref[...]
    lane = lax.broadcasted_iota(jnp.int32, z.shape, 1)
    return jnp.where(lane < N_HEADS, _log_sigmoid(z), 0.0)


def _store_layer(ref, val):
    if len(ref.shape) == val.ndim:
        ref[...] = val
    else:
        ref[0] = val
        for other in range(1, ref.shape[0]):
            ref[other] = jnp.zeros_like(val)


def _fox_proj_prompt_kernel(x_ref, g_ref, w_ref, wvt_ref, bf_ref, pq_ref, pk_ref, oq_ref, ok_ref, *refs, tm):
    kt_ref, vt_ref, lft_ref, qa_ref, ka_ref, vtb_ref, carry_ref = refs[-7:]
    i = pl.program_id(1)

    @pl.when(i == 0)
    def _():
        carry_ref[...] = jnp.zeros_like(carry_ref)

    h = _rmsnorm(x_ref[...], g_ref[...]).astype(BF16)
    qk = jnp.dot(h, w_ref[...], preferred_element_type=F32)
    q = qk[:, 0:D_MODEL] * Q_SCALE
    k = qk[:, D_MODEL:2 * D_MODEL]
    lf = _gate_logits(qk[:, 2 * D_MODEL:2 * D_MODEL + LANES], bf_ref)
    vt = lax.dot_general(wvt_ref[...], h, (((1,), (1,)), ((), ())), preferred_element_type=F32)
    _store_layer(kt_ref, k.T)
    _store_layer(vt_ref, vt)
    vtb_ref[...] = vt.astype(BF16)
    lft_ref[...] = lf.T[0:N_HEADS, :]

    row = lax.broadcasted_iota(jnp.int32, (tm, tm), 0)
    col = lax.broadcasted_iota(jnp.int32, (tm, tm), 1)
    tri = (row >= col).astype(BF16)
    f = carry_ref[...]
    for piece in _split_bf16(lf * LOG2E):
        f = f + jnp.dot(tri, piece, preferred_element_type=F32)
    carry_ref[...] = f[tm - 1:tm, :]

    packed = None
    for p, piece in enumerate(_split_bf16(f)):
        shifted = piece.astype(F32)
        if p:
            shifted = pltpu.roll(shifted, p * N_HEADS, axis=1)
        packed = shifted if packed is None else packed + shifted
    packed = packed.astype(BF16)
    aug_q = jnp.dot(packed, pq_ref[...], preferred_element_type=F32) + oq_ref[...]
    aug_k = jnp.dot(-packed, pk_ref[...], preferred_element_type=F32) + ok_ref[...]

    lane = lax.broadcasted_iota(jnp.int32, (tm, LANES), 1)
    for hd in range(N_HEADS):
        slab = slice((hd // 2) * LANES, (hd // 2 + 1) * LANES)
        keep = (lane < HEAD_DIM) if hd % 2 == 0 else (lane >= HEAD_DIM)
        qa_ref[hd] = jnp.where(keep, q[:, slab], aug_q[:, slab]).astype(BF16)
        ka_ref[hd] = jnp.where(keep, k[:, slab], aug_k[:, slab]).astype(BF16)


def _fox_proj_prompt(x, g, wqk, wvt, bfp, layer, n_layers, prev_kv):
    b, t, d = x.shape
    tm = PROJ_ROWS
    n_tiles = t // tm
    pq, pk, oq, ok = _aug_constants()
    rows = lambda bi, i: (bi, i, 0)
    cols = lambda bi, i: (bi, 0, i)
    if prev_kv is None:
        assert layer == 0
        prev_kv, aliases = [], {}
        kv_spec = pl.BlockSpec((n_layers, None, d, tm), lambda bi, i: (0, bi, 0, i))
    else:
        aliases = {9: 0, 10: 1}
        kv_spec = pl.BlockSpec((None, None, d, tm), lambda bi, i: (layer, bi, 0, i))
    return pl.pallas_call(
        functools.partial(_fox_proj_prompt_kernel, tm=tm),
        grid=(b, n_tiles),
        in_specs=[
            pl.BlockSpec((None, tm, d), rows),
            _resident((1, d)),
            _resident(wqk.shape),
            _resident(wvt.shape),
            _resident((1, LANES)),
            _resident(pq.shape),
            _resident(pk.shape),
            _resident(oq.shape),
            _resident(ok.shape),
        ] + [pl.BlockSpec(memory_space=pl.ANY)] * len(prev_kv),
        input_output_aliases=aliases,
        out_specs=[
            kv_spec,
            kv_spec,
            pl.BlockSpec((None, N_HEADS, tm), cols),
            pl.BlockSpec((None, N_HEADS, tm, LANES), lambda bi, i: (bi, 0, i, 0)),
            pl.BlockSpec((None, N_HEADS, tm, LANES), lambda bi, i: (bi, 0, i, 0)),
            pl.BlockSpec((None, None, d, tm), lambda bi, i: (bi, i, 0, 0)),
        ],
        out_shape=[
            jax.ShapeDtypeStruct((n_layers, b, d, t), F32),
            jax.ShapeDtypeStruct((n_layers, b, d, t), F32),
            jax.ShapeDtypeStruct((b, N_HEADS, t), F32),
            jax.ShapeDtypeStruct((b, N_HEADS, t, LANES), BF16),
            jax.ShapeDtypeStruct((b, N_HEADS, t, LANES), BF16),
            jax.ShapeDtypeStruct((b, n_tiles, d, tm), BF16),
        ],
        scratch_shapes=[pltpu.VMEM((1, LANES), F32)],
        compiler_params=_params("arbitrary", "arbitrary"),
        name="fox_proj_prompt",
    )(x, g.reshape(1, d), wqk, wvt, bfp, pq, pk, oq, ok, *prev_kv)


def _fox_proj_sample_kernel(x_ref, g_ref, w_ref, wvt_ref, bf_ref, q_ref, k_ref, v_ref, lf_ref):
    h = _rmsnorm(x_ref[...], g_ref[...]).astype(BF16)
    qk = jnp.dot(h, w_ref[...], preferred_element_type=F32)
    q_ref[...] = qk[:, 0:D_MODEL] * Q_SCALE
    k_ref[...] = qk[:, D_MODEL:2 * D_MODEL]
    v_ref[...] = lax.dot_general(h, wvt_ref[...], (((1,), (1,)), ((), ())), preferred_element_type=F32)
    lf_ref[...] = _gate_logits(qk[:, 2 * D_MODEL:2 * D_MODEL + LANES], bf_ref)[:, 0:N_HEADS]


def _fox_proj_sample(x, g, wqk, wvt, bfp):
    bs, d = x.shape
    return pl.pallas_call(
        _fox_proj_sample_kernel,
        out_shape=[
            jax.ShapeDtypeStruct((bs, d), F32),
            jax.ShapeDtypeStruct((bs, d), F32),
            jax.ShapeDtypeStruct((bs, d), F32),
            jax.ShapeDtypeStruct((bs, N_HEADS), F32),
        ],
        compiler_params=pltpu.CompilerParams(vmem_limit_bytes=VMEM_LIMIT),
        name="fox_proj_sample",
    )(x, g.reshape(1, d), wqk, wvt, bfp)


def _prompt_tile_attention(i, q_ref, k_ref, vt_ref, o_ref, m_ref, acc_ref, *, tq, tk, nh):
    n_sub = tq // tk
    m_ref[...] = jnp.full_like(m_ref, -jnp.inf)
    acc_ref[...] = jnp.zeros_like(acc_ref)
    ones_rows = jnp.ones((SUM_ROWS, tk), BF16)

    def scores(hh, it):
        kb = k_ref[hh, pl.ds(pl.multiple_of(it * tq, tq), tq), :]
        return lax.dot_general(kb, q_ref[hh], (((1,), (1,)), ((), ())), preferred_element_type=F32)

    def update(hh, it, s):
        m_old = m_ref[hh]
        m_new = jnp.maximum(m_old, jnp.max(s, axis=0, keepdims=True))
        pb = jnp.exp2(s - m_new).astype(BF16)
        acc = jnp.exp2(m_old - m_new) * acc_ref[hh]
        for c in range(n_sub):
            vt = vt_ref[it * n_sub + c, hh * HEAD_DIM:(hh + 1) * HEAD_DIM, :]
            lhs = jnp.concatenate([vt, ones_rows], axis=0)
            acc = acc + jnp.dot(lhs, pb[c * tk:(c + 1) * tk, :], preferred_element_type=F32)
        acc_ref[hh] = acc
        m_ref[hh] = m_new

    def tile(it, mask):
        s_next = scores(0, it)
        for hh in range(nh):
            s = s_next
            if hh + 1 < nh:
                s_next = scores(hh + 1, it)
            if mask is not None:
                s = jnp.where(mask, s, -jnp.inf)
            update(hh, it, s)

    def body(it, carry):
        tile(it, None)
        return carry

    lax.fori_loop(0, i, body, 0)
    kv_local = lax.broadcasted_iota(jnp.int32, (tq, tq), 0)
    q_local = lax.broadcasted_iota(jnp.int32, (tq, tq), 1)
    tile(i, kv_local <= q_local)
    out = jnp.concatenate(
        [acc_ref[hh, 0:HEAD_DIM, :] / acc_ref[hh, HEAD_DIM:HEAD_DIM + 1, :] for hh in range(nh)], axis=0)
    o_ref[...] = out.T.astype(BF16)


def _head_sums(x):
    return jnp.sum(x.reshape(N_HEADS, HEAD_DIM, x.shape[1]), axis=1)


def _sample_chunk_attention(c, q_ref, kn_ref, vn_ref, lfn_ref, k_refs, v_refs, lf_refs,
                            o_ref, qb_ref, z_ref, m_ref, l_ref, acc_ref, fc_ref, *, n_chunks):
    n_pages = len(k_refs)

    @pl.when(c == 0)
    def _():
        qb_ref[...] = _lane_broadcast_column(q_ref[...])
        m_ref[...] = jnp.full_like(m_ref, -jnp.inf)
        l_ref[...] = jnp.zeros_like(l_ref)
        acc_ref[...] = jnp.zeros_like(acc_ref)
        fc_ref[...] = jnp.zeros_like(fc_ref)

    for hd in range(N_HEADS):
        rows = slice(hd * HEAD_DIM, (hd + 1) * HEAD_DIM)
        qh = qb_ref[rows, :]
        for r_ in range(n_pages):
            z_ref[hd:hd + 1, r_ * PAGE_SIZE:(r_ + 1) * PAGE_SIZE] = jnp.sum(
                k_refs[r_][rows, :] * qh, axis=0, keepdims=True)

    row = lax.broadcasted_iota(jnp.int32, (PAGE_SIZE, PAGE_SIZE), 0)
    col = lax.broadcasted_iota(jnp.int32, (PAGE_SIZE, PAGE_SIZE), 1)
    upper = (row <= col).astype(BF16)
    f_run = fc_ref[...]
    fs = []
    for r_ in range(n_pages):
        f = f_run
        for piece in _split_bf16(lf_refs[r_][...] * LOG2E):
            f = f + jnp.dot(piece, upper, preferred_element_type=F32)
        f_run = f[:, PAGE_SIZE - 1:PAGE_SIZE]
        fs.append(f)
    fc_ref[...] = f_run
    z = z_ref[...] - jnp.concatenate(fs, axis=1)

    m_old = m_ref[...]
    m_new = jnp.maximum(m_old, jnp.max(z, axis=1, keepdims=True))
    p = jnp.exp2(z - m_new)
    alpha = jnp.exp2(m_old - m_new)
    l_new = alpha * l_ref[...] + jnp.sum(p, axis=1, keepdims=True)
    for hd in range(N_HEADS):
        rows = slice(hd * HEAD_DIM, (hd + 1) * HEAD_DIM)
        a = acc_ref[rows, :] * alpha[hd:hd + 1, :]
        for r_ in range(n_pages):
            a = a + v_refs[r_][rows, :] * p[hd:hd + 1, r_ * PAGE_SIZE:(r_ + 1) * PAGE_SIZE]
        acc_ref[rows, :] = a
    m_ref[...] = m_new
    l_ref[...] = l_new

    @pl.when(c == n_chunks - 1)
    def _():
        s_new = _head_sums(_lane_broadcast_column(kn_ref[...]) * qb_ref[...])[:, 0:1]
        z_new = s_new - (f_run + lfn_ref[...] * LOG2E)
        m_fin = jnp.maximum(m_new, z_new)
        p_new = jnp.exp2(z_new - m_fin)
        a_fin = jnp.exp2(m_new - m_fin)
        inv_l = 1.0 / (a_fin * l_new + p_new)
        vnb = _lane_broadcast_column(vn_ref[...])
        lane0 = lax.broadcasted_iota(jnp.int32, (HEAD_DIM, LANES), 1) == 0
        for hd in range(N_HEADS):
            rows = slice(hd * HEAD_DIM, (hd + 1) * HEAD_DIM)
            a = acc_ref[rows, :] * a_fin[hd:hd + 1, :]
            a = a + jnp.where(lane0, vnb[rows, :] * p_new[hd:hd + 1, :], 0.0)
            acc_ref[rows, :] = a * inv_l[hd:hd + 1, :]
        o_ref[...] = jnp.sum(acc_ref[...].T, axis=0, keepdims=True).astype(BF16)


def _attn_kernel(pt_ref, q_ref, k_ref, vt_ref, qs_ref, kn_ref, vn_ref, lfn_ref, *refs,
                 tq, tk, nh, n_pages, n_chunks):
    del pt_ref
    k_refs = refs[0:n_pages]
    v_refs = refs[n_pages:2 * n_pages]
    lf_refs = refs[2 * n_pages:3 * n_pages]
    o_ref, os_ref, pm_ref, pacc_ref, qb_ref, z_ref, sm_ref, sl_ref, sacc_ref, fc_ref = refs[3 * n_pages:]
    i = pl.program_id(2)
    _sample_chunk_attention(lax.rem(i, n_chunks), qs_ref, kn_ref, vn_ref, lfn_ref, k_refs, v_refs, lf_refs,
                            os_ref, qb_ref, z_ref, sm_ref, sl_ref, sacc_ref, fc_ref, n_chunks=n_chunks)
    _prompt_tile_attention(i, q_ref, k_ref, vt_ref, o_ref, pm_ref, pacc_ref, tq=tq, tk=tk, nh=nh)


def _attn(qa, ka, vtb, page_table, q, kn, vn, lfn, cache_kt, cache_vt, cache_lft, layer):
    b, n_heads, t, _ = qa.shape
    n_kv, d, tk = vtb.shape[1], vtb.shape[2], vtb.shape[3]
    tq = ATTN_Q
    nh = ATTN_HEADS
    ng, nq = n_heads // nh, t // tq
    bs = q.shape[0]
    n_log = page_table.shape[1]
    n_chunks, rem = divmod(b * ng * nq, bs)
    assert rem == 0 and nq % n_chunks == 0 and n_log % n_chunks == 0
    n_pages = n_log // n_chunks
    seqs_per_row = nq // n_chunks

    def seq(bi, j, i):
        return (bi * ng + j) * seqs_per_row + i // n_chunks

    def page_map(r_):
        return lambda bi, j, i, pt: (layer, pt[seq(bi, j, i), lax.rem(i, n_chunks) * n_pages + r_], 0, 0)

    tok = lambda bi, j, i, pt: (seq(bi, j, i), 0, 0)
    in_specs = [
        pl.BlockSpec((None, nh, tq, LANES), lambda bi, j, i, pt: (bi, j, i, 0)),
        pl.BlockSpec((None, nh, t, LANES), lambda bi, j, i, pt: (bi, j, 0, 0)),
        pl.BlockSpec((None, n_kv, nh * HEAD_DIM, tk), lambda bi, j, i, pt: (bi, 0, j, 0)),
    ]
    in_specs += [pl.BlockSpec((None, 1, d), tok)] * 3 + [pl.BlockSpec((None, N_HEADS, 1), tok)]
    in_specs += [pl.BlockSpec((None, None, d, PAGE_SIZE), page_map(r_)) for r_ in range(n_pages)]
    in_specs += [pl.BlockSpec((None, None, d, PAGE_SIZE), page_map(r_)) for r_ in range(n_pages)]
    in_specs += [pl.BlockSpec((None, None, N_HEADS, PAGE_SIZE), page_map(r_)) for r_ in range(n_pages)]
    grid_spec = pltpu.PrefetchScalarGridSpec(
        num_scalar_prefetch=1,
        grid=(b, ng, nq),
        in_specs=in_specs,
        out_specs=[
            pl.BlockSpec((None, tq, nh * HEAD_DIM), lambda bi, j, i, pt: (bi, i, j)),
            pl.BlockSpec((None, 1, d), tok),
        ],
        scratch_shapes=[
            pltpu.VMEM((nh, 1, tq), F32),
            pltpu.VMEM((nh, HEAD_DIM + SUM_ROWS, tq), F32),
            pltpu.VMEM((d, LANES), F32),
            pltpu.VMEM((N_HEADS, n_pages * PAGE_SIZE), F32),
            pltpu.VMEM((N_HEADS, 1), F32),
            pltpu.VMEM((N_HEADS, 1), F32),
            pltpu.VMEM((d, LANES), F32),
            pltpu.VMEM((N_HEADS, 1), F32),
        ],
    )
    o, o_s = pl.pallas_call(
        functools.partial(_attn_kernel, tq=tq, tk=tk, nh=nh, n_pages=n_pages, n_chunks=n_chunks),
        grid_spec=grid_spec,
        out_shape=[jax.ShapeDtypeStruct((b, t, d), BF16), jax.ShapeDtypeStruct((bs, 1, d), BF16)],
        compiler_params=_params("arbitrary", "arbitrary", "arbitrary"),
        name="attn",
    )(page_table, qa, ka, vtb, q.reshape(bs, 1, d), kn.reshape(bs, 1, d), vn.reshape(bs, 1, d),
      lfn.reshape(bs, N_HEADS, 1), *([cache_kt] * n_pages), *([cache_vt] * n_pages),
      *([cache_lft] * n_pages))
    return o, o_s.reshape(bs, d)


def _feature_major_pages(cache):
    if cache.ndim == 5:
        n_layers, n_phys, page, nh, hd = cache.shape
        return jnp.transpose(cache, (0, 1, 3, 4, 2)).reshape(n_layers, n_phys, nh * hd, page)
    return jnp.transpose(cache, (0, 1, 3, 2))


def kernel(x_prompt, x_sample, cache_k, cache_v, cache_logf, state_pool, page_table, norm_mix, norm_mlp,
           norm_final, pool_w, pool_scale, w_qkv, w_f, b_f, w_o, w_up, w_down):
    b, t, d = x_prompt.shape
    bs, ts, _ = x_sample.shape
    assert ts == 1 and d == D_MODEL
    depth = norm_mix.shape[0]
    past = page_table.shape[1] * PAGE_SIZE

    ckt = _feature_major_pages(cache_k)
    cvt = _feature_major_pages(cache_v)
    clft = _feature_major_pages(cache_logf)
    ones = jnp.ones((d,), F32)

    xp = x_prompt.reshape(b * t, d)
    xs = x_sample.reshape(bs, d)
    kv_t, lp, pp = None, [], []
    ksl, vsl, lsl, psl = [], [], [], []
    for i in range(depth):
        j = i // 2
        if i % 2 == 0:
            a_p, buf = _pool_prompt(xp.reshape(b, t, d), norm_mix[i])
            a_p = a_p.reshape(b * t, d)
            pp.append(buf[:, HALO - POOL_BUF:, :])
            a_s, nst = _pool_sample(xs, jnp.transpose(state_pool[j], (1, 0, 2)), norm_mix[i], past)
            psl.append(jnp.transpose(nst, (1, 0, 2)))
            wm = pool_w[j].astype(BF16)
            sc = pool_scale[j]
        else:
            wqk = jnp.concatenate(
                [w_qkv[j, :, 0:2 * d], jnp.pad(w_f[j], ((0, 0), (0, LANES - N_HEADS)))], axis=1).astype(BF16)
            wvt = w_qkv[j, :, 2 * d:3 * d].T.astype(BF16)
            bfp = jnp.pad(b_f[j], (0, LANES - N_HEADS)).reshape(1, LANES)
            *kv_t, lft, qa, ka, vtb = _fox_proj_prompt(
                xp.reshape(b, t, d), norm_mix[i], wqk, wvt, bfp, j, depth // 2, kv_t)
            lp.append(jnp.transpose(lft, (0, 2, 1)))
            q2, k2, v2, l2 = _fox_proj_sample(xs, norm_mix[i], wqk, wvt, bfp)
            a_p, a_s = _attn(qa, ka, vtb, page_table, q2, k2, v2, l2, ckt, cvt, clft, j)
            a_p = a_p.reshape(b * t, d)
            ksl.append(k2.reshape(bs, 1, N_HEADS, HEAD_DIM))
            vsl.append(v2.reshape(bs, 1, N_HEADS, HEAD_DIM))
            lsl.append(l2.reshape(bs, 1, N_HEADS))
            wm = w_o[j].astype(BF16)
            sc = ones
        wu = w_up[i].astype(BF16)
        wd = w_down[i].astype(BF16)
        g_out = norm_final if i == depth - 1 else None
        xp = _mix_mlp(xp, a_p, wm, sc, norm_mlp[i], wu, wd, g_out)
        xs = _mix_mlp(xs, a_s, wm, sc, norm_mlp[i], wu, wd, g_out)
    y_prompt = xp.reshape(b, t, d)
    y_sample = xs.reshape(bs, 1, d)
    n_fox = kv_t[0].shape[0]
    kp, vp = [jnp.transpose(a.reshape(n_fox, b, N_HEADS, HEAD_DIM, t), (0, 1, 4, 2, 3)) for a in kv_t]
    return (y_prompt, y_sample,
            kp, vp, jnp.stack(lp), jnp.stack(pp),
            jnp.stack(ksl), jnp.stack(vsl), jnp.stack(lsl), jnp.stack(psl))
```

```python
import functools

import numpy as np
import jax
import jax.numpy as jnp
from jax import lax
from jax.experimental import pallas as pl
from jax.experimental.pallas import tpu as pltpu

F32 = jnp.float32
BF16 = jnp.bfloat16

D_MODEL = 1024
N_HEADS = 16
HEAD_DIM = 64
D_FF = 4 * D_MODEL
POOL_WINDOWS = (2, 4, 8, 16)
POOL_GROUP = D_MODEL // len(POOL_WINDOWS)
POOL_BUF = max(POOL_WINDOWS) - 1
PAGE_SIZE = 128
RMS_EPS = 1e-6

LANES = 128
HALO = 16
VMEM_LIMIT = 60 * 1024 * 1024

LOG2E = 1.4426950408889634
Q_SCALE = HEAD_DIM ** -0.5 * LOG2E

MLP_ROWS = 512
MLP_FF_CHUNK = 1024
POOL_ROWS = 512
PROJ_ROWS = 256
ATTN_Q = 512
ATTN_HEADS = 4
SAMPLE_STAGES = 2
SUM_ROWS = 16
N_SPLIT = 3


def _rmsnorm(x, g):
    r = lax.rsqrt(jnp.mean(x * x, axis=-1, keepdims=True) + RMS_EPS)
    return (x * r) * g


def _split_bf16(x):
    pieces = []
    rem = x
    for _ in range(N_SPLIT):
        p = rem.astype(BF16)
        pieces.append(p)
        rem = rem - p.astype(F32)
    return pieces


def _log_sigmoid(z):
    return jnp.minimum(z, 0.0) - jnp.log1p(jnp.exp(-jnp.abs(z)))


def _lane_broadcast_column(row):
    return jnp.broadcast_to(row, (LANES, row.shape[1])).T


def _params(*semantics):
    return pltpu.CompilerParams(dimension_semantics=semantics, vmem_limit_bytes=VMEM_LIMIT)


def _resident(shape):
    return pl.BlockSpec(shape, lambda *_: (0,) * len(shape), pipeline_mode=pl.Buffered(1))


def _pool_prompt_kernel(x_ref, halo_ref, g_ref, a_ref, buf_ref, *lvl_refs, tm, n_tiles):
    i = pl.program_id(1)
    g = g_ref[...]
    h = _rmsnorm(x_ref[...], g)
    hh = _rmsnorm(halo_ref[...], g)
    top = 2 * HALO
    end = top + tm
    e0 = lvl_refs[0]
    e0[0:HALO, :] = jnp.zeros((HALO, h.shape[1]), F32)
    e0[HALO:top, :] = jnp.where(i == 0, 0.0, hh)
    e0[top:end, :] = h
    pos = i * tm + lax.broadcasted_iota(jnp.int32, (tm, 1), 0)
    outs = []
    prev = e0
    for lvl, w in enumerate(POOL_WINDOWS):
        first = 8 * (lvl + 1)
        shift = w // 2
        drop = POOL_GROUP if lvl else 0
        if lvl + 1 < len(POOL_WINDOWS):
            cur = lvl_refs[lvl + 1]
            cur[first:end, :] = prev[first:end, drop:] + prev[first - shift:end - shift, drop:]
            s = cur[top:end, 0:POOL_GROUP]
            prev = cur
        else:
            s = prev[top:end, drop:] + prev[top - shift:end - shift, drop:]
        cnt = jnp.minimum(w, pos + 1).astype(F32)
        outs.append(s / cnt)
    pooled = jnp.concatenate(outs, axis=-1)
    a_ref[...] = (pooled - h).astype(BF16)

    @pl.when(i == n_tiles - 1)
    def _():
        buf_ref[...] = e0[end - HALO:end, :]


def _pool_prompt(x, g):
    b, t, d = x.shape
    tm = POOL_ROWS
    n_tiles = t // tm
    per = tm // HALO
    return pl.pallas_call(
        functools.partial(_pool_prompt_kernel, tm=tm, n_tiles=n_tiles),
        grid=(b, n_tiles),
        in_specs=[
            pl.BlockSpec((None, tm, d), lambda bi, i: (bi, i, 0)),
            pl.BlockSpec((None, HALO, d), lambda bi, i: (bi, jnp.maximum(i * per - 1, 0), 0)),
            pl.BlockSpec((1, d), lambda bi, i: (0, 0)),
        ],
        out_specs=[
            pl.BlockSpec((None, tm, d), lambda bi, i: (bi, i, 0)),
            pl.BlockSpec((None, HALO, d), lambda bi, i: (bi, 0, 0)),
        ],
        out_shape=[
            jax.ShapeDtypeStruct((b, t, d), BF16),
            jax.ShapeDtypeStruct((b, HALO, d), F32),
        ],
        scratch_shapes=[pltpu.VMEM((2 * HALO + tm, d - max(lvl - 1, 0) * POOL_GROUP), F32)
                        for lvl in range(len(POOL_WINDOWS))],
        compiler_params=_params("arbitrary", "arbitrary"),
        name="pool_prompt",
    )(x, x, g.reshape(1, d))


def _pool_sample_kernel(x_ref, st_ref, g_ref, a_ref, nst_ref, *, pos):
    h = _rmsnorm(x_ref[...], g_ref[...])
    outs = []
    for gi, w in enumerate(POOL_WINDOWS):
        lo, hi = gi * POOL_GROUP, (gi + 1) * POOL_GROUP
        s = h[:, lo:hi]
        for j in range(1, w):
            s = s + st_ref[POOL_BUF - j, :, lo:hi]
        outs.append(s / float(min(w, pos + 1)))
    pooled = jnp.concatenate(outs, axis=-1)
    a_ref[...] = (pooled - h).astype(BF16)
    for r in range(POOL_BUF - 1):
        nst_ref[r] = st_ref[r + 1]
    nst_ref[POOL_BUF - 1] = h


def _pool_sample(x, state_t, g, pos):
    bs, d = x.shape
    return pl.pallas_call(
        functools.partial(_pool_sample_kernel, pos=pos),
        out_shape=[
            jax.ShapeDtypeStruct((bs, d), BF16),
            jax.ShapeDtypeStruct((POOL_BUF, bs, d), F32),
        ],
        compiler_params=pltpu.CompilerParams(vmem_limit_bytes=VMEM_LIMIT),
        name="pool_sample",
    )(x, state_t, g.reshape(1, d))


def _mix_mlp_kernel(x_ref, a_ref, wm_ref, sc_ref, g_ref, wu_ref, wd_ref, *rest):
    o_ref = rest[-1]
    if len(wm_ref.shape) == 3:
        gw = wm_ref.shape[1]
        y = jnp.concatenate(
            [jnp.dot(a_ref[:, gi * gw:(gi + 1) * gw], wm_ref[gi], preferred_element_type=F32)
             for gi in range(wm_ref.shape[0])], axis=-1)
    else:
        y = jnp.dot(a_ref[...], wm_ref[...], preferred_element_type=F32)
    x1 = x_ref[...] + y * sc_ref[...]
    h = _rmsnorm(x1, g_ref[...]).astype(BF16)
    acc = x1
    for c in range(D_FF // MLP_FF_CHUNK):
        cols = slice(c * MLP_FF_CHUNK, (c + 1) * MLP_FF_CHUNK)
        u = jnp.maximum(jnp.dot(h, wu_ref[:, cols], preferred_element_type=F32), 0.0)
        acc = acc + jnp.dot((u * u).astype(BF16), wd_ref[cols, :], preferred_element_type=F32)
    o_ref[...] = _rmsnorm(acc, rest[0][...]) if len(rest) == 2 else acc


def _mix_mlp(x, a, wm, sc, g, wu, wd, g_out=None):
    n, d = x.shape
    tm = min(MLP_ROWS, n)
    rows = lambda i: (i, 0)
    gains = [g.reshape(1, d)] + ([] if g_out is None else [g_out.reshape(1, d)])
    return pl.pallas_call(
        _mix_mlp_kernel,
        grid=(n // tm,),
        in_specs=[
            pl.BlockSpec((tm, d), rows),
            pl.BlockSpec((tm, d), rows),
            _resident(wm.shape),
            _resident((1, d)),
            _resident((1, d)),
            _resident((d, D_FF)),
            _resident((D_FF, d)),
        ] + [_resident((1, d))] * (len(gains) - 1),
        out_specs=pl.BlockSpec((tm, d), rows),
        out_shape=jax.ShapeDtypeStruct((n, d), F32),
        compiler_params=_params("arbitrary"),
        name="mix_mlp",
    )(x, a, wm, sc.reshape(1, d), gains[0], wu, wd, *gains[1:])


def _aug_constants():
    width = (N_HEADS // 2) * LANES
    pq = np.zeros((LANES, width), np.float32)
    pk = np.zeros((LANES, width), np.float32)
    oq = np.zeros((1, width), np.float32)
    ok = np.zeros((1, width), np.float32)
    for h in range(N_HEADS):
        base = (h // 2) * LANES + (HEAD_DIM if h % 2 == 0 else 0)
        for p in range(N_SPLIT):
            oq[0, base + p] = 1.0
            pq[p * N_HEADS + h, base + N_SPLIT + p] = 1.0
            pk[p * N_HEADS + h, base + p] = 1.0
            ok[0, base + N_SPLIT + p] = 1.0
    return (jnp.asarray(pq, BF16), jnp.asarray(pk, BF16), jnp.asarray(oq), jnp.asarray(ok))


def _gate_logits(z, bf_ref):
    z = z + bf_ref[...]
    lane = lax.broadcasted_iota(jnp.int32, z.shape, 1)
    return jnp.where(lane < N_HEADS, _log_sigmoid(z), 0.0)


def _store_layer(ref, val):
    if len(ref.shape) == val.ndim:
        ref[...] = val
    else:
        ref[0] = val
        for other in range(1, ref.shape[0]):
            ref[other] = jnp.zeros_like(val)


def _fox_proj_prompt_kernel(x_ref, g_ref, w_ref, wvt_ref, bf_ref, pq_ref, pk_ref, oq_ref, ok_ref, *refs, tm):
    kt_ref, vt_ref, lft_ref, qa_ref, ka_ref, vtb_ref, carry_ref = refs[-7:]
    i = pl.program_id(1)

    @pl.when(i == 0)
    def _():
        carry_ref[...] = jnp.zeros_like(carry_ref)

    h = _rmsnorm(x_ref[...], g_ref[...]).astype(BF16)
    qk = jnp.dot(h, w_ref[...], preferred_element_type=F32)
    q = qk[:, 0:D_MODEL] * Q_SCALE
    k = qk[:, D_MODEL:2 * D_MODEL]
    lf = _gate_logits(qk[:, 2 * D_MODEL:2 * D_MODEL + LANES], bf_ref)
    vt = lax.dot_general(wvt_ref[...], h, (((1,), (1,)), ((), ())), preferred_element_type=F32)
    _store_layer(kt_ref, k.T)
    _store_layer(vt_ref, vt)
    vtb_ref[...] = vt.astype(BF16)
    lft_ref[...] = lf.T[0:N_HEADS, :]

    row = lax.broadcasted_iota(jnp.int32, (tm, tm), 0)
    col = lax.broadcasted_iota(jnp.int32, (tm, tm), 1)
    tri = (row >= col).astype(BF16)
    f = carry_ref[...]
    for piece in _split_bf16(lf * LOG2E):
        f = f + jnp.dot(tri, piece, preferred_element_type=F32)
    carry_ref[...] = f[tm - 1:tm, :]

    packed = None
    for p, piece in enumerate(_split_bf16(f)):
        shifted = piece.astype(F32)
        if p:
            shifted = pltpu.roll(shifted, p * N_HEADS, axis=1)
        packed = shifted if packed is None else packed + shifted
    packed = packed.astype(BF16)
    aug_q = jnp.dot(packed, pq_ref[...], preferred_element_type=F32) + oq_ref[...]
    aug_k = jnp.dot(-packed, pk_ref[...], preferred_element_type=F32) + ok_ref[...]

    lane = lax.broadcasted_iota(jnp.int32, (tm, LANES), 1)
    for hd in range(N_HEADS):
        slab = slice((hd // 2) * LANES, (hd // 2 + 1) * LANES)
        keep = (lane < HEAD_DIM) if hd % 2 == 0 else (lane >= HEAD_DIM)
        qa_ref[hd] = jnp.where(keep, q[:, slab], aug_q[:, slab]).astype(BF16)
        ka_ref[hd] = jnp.where(keep, k[:, slab], aug_k[:, slab]).astype(BF16)


def _fox_proj_prompt(x, g, wqk, wvt, bfp, layer, n_layers, prev_kv):
    b, t, d = x.shape
    tm = PROJ_ROWS
    n_tiles = t // tm
    pq, pk, oq, ok = _aug_constants()
    rows = lambda bi, i: (bi, i, 0)
    cols = lambda bi, i: (bi, 0, i)
    if prev_kv is None:
        assert layer == 0
        prev_kv, aliases = [], {}
        kv_spec = pl.BlockSpec((n_layers, None, d, tm), lambda bi, i: (0, bi, 0, i))
    else:
        aliases = {9: 0, 10: 1}
        kv_spec = pl.BlockSpec((None, None, d, tm), lambda bi, i: (layer, bi, 0, i))
    return pl.pallas_call(
        functools.partial(_fox_proj_prompt_kernel, tm=tm),
        grid=(b, n_tiles),
        in_specs=[
            pl.BlockSpec((None, tm, d), rows),
            _resident((1, d)),
            _resident(wqk.shape),
            _resident(wvt.shape),
            _resident((1, LANES)),
            _resident(pq.shape),
            _resident(pk.shape),
            _resident(oq.shape),
            _resident(ok.shape),
        ] + [pl.BlockSpec(memory_space=pl.ANY)] * len(prev_kv),
        input_output_aliases=aliases,
        out_specs=[
            kv_spec,
            kv_spec,
            pl.BlockSpec((None, N_HEADS, tm), cols),
            pl.BlockSpec((None, N_HEADS, tm, LANES), lambda bi, i: (bi, 0, i, 0)),
            pl.BlockSpec((None, N_HEADS, tm, LANES), lambda bi, i: (bi, 0, i, 0)),
            pl.BlockSpec((None, None, d, tm), lambda bi, i: (bi, i, 0, 0)),
        ],
        out_shape=[
            jax.ShapeDtypeStruct((n_layers, b, d, t), F32),
            jax.ShapeDtypeStruct((n_layers, b, d, t), F32),
            jax.ShapeDtypeStruct((b, N_HEADS, t), F32),
            jax.ShapeDtypeStruct((b, N_HEADS, t, LANES), BF16),
            jax.ShapeDtypeStruct((b, N_HEADS, t, LANES), BF16),
            jax.ShapeDtypeStruct((b, n_tiles, d, tm), BF16),
        ],
        scratch_shapes=[pltpu.VMEM((1, LANES), F32)],
        compiler_params=_params("arbitrary", "arbitrary"),
        name="fox_proj_prompt",
    )(x, g.reshape(1, d), wqk, wvt, bfp, pq, pk, oq, ok, *prev_kv)


def _fox_proj_sample_kernel(x_ref, g_ref, w_ref, wvt_ref, bf_ref, q_ref, k_ref, v_ref, lf_ref):
    h = _rmsnorm(x_ref[...], g_ref[...]).astype(BF16)
    qk = jnp.dot(h, w_ref[...], preferred_element_type=F32)
    q_ref[...] = qk[:, 0:D_MODEL] * Q_SCALE
    k_ref[...] = qk[:, D_MODEL:2 * D_MODEL]
    v_ref[...] = lax.dot_general(h, wvt_ref[...], (((1,), (1,)), ((), ())), preferred_element_type=F32)
    lf_ref[...] = _gate_logits(qk[:, 2 * D_MODEL:2 * D_MODEL + LANES], bf_ref)[:, 0:N_HEADS]


def _fox_proj_sample(x, g, wqk, wvt, bfp):
    bs, d = x.shape
    return pl.pallas_call(
        _fox_proj_sample_kernel,
        out_shape=[
            jax.ShapeDtypeStruct((bs, d), F32),
            jax.ShapeDtypeStruct((bs, d), F32),
            jax.ShapeDtypeStruct((bs, d), F32),
            jax.ShapeDtypeStruct((bs, N_HEADS), F32),
        ],
        compiler_params=pltpu.CompilerParams(vmem_limit_bytes=VMEM_LIMIT),
        name="fox_proj_sample",
    )(x, g.reshape(1, d), wqk, wvt, bfp)


def _prompt_tile_attention(i, q_ref, k_ref, vt_ref, o_ref, m_ref, acc_ref, s0_ref, *, tq, tk, nh, overlap=None):
    n_sub = tq // tk
    m_ref[...] = jnp.full_like(m_ref, -jnp.inf)
    acc_ref[...] = jnp.zeros_like(acc_ref)
    ones_rows = jnp.ones((SUM_ROWS, tk), BF16)

    def scores(hh, it):
        kb = k_ref[hh, pl.ds(pl.multiple_of(it * tq, tq), tq), :]
        return lax.dot_general(kb, q_ref[hh], (((1,), (1,)), ((), ())), preferred_element_type=F32)

    def update(hh, it, s):
        m_old = m_ref[hh]
        m_new = jnp.maximum(m_old, jnp.max(s, axis=0, keepdims=True))
        pb = jnp.exp2(s - m_new).astype(BF16)
        acc = jnp.exp2(m_old - m_new) * acc_ref[hh]
        for c in range(n_sub):
            vt = vt_ref[it * n_sub + c, hh * HEAD_DIM:(hh + 1) * HEAD_DIM, :]
            lhs = jnp.concatenate([vt, ones_rows], axis=0)
            acc = acc + jnp.dot(lhs, pb[c * tk:(c + 1) * tk, :], preferred_element_type=F32)
        acc_ref[hh] = acc
        m_ref[hh] = m_new

    def tile(it, mask, between=None, last=False):
        s_next = s0_ref[...]
        for hh in range(nh):
            s = s_next
            if hh + 1 < nh:
                s_next = scores(hh + 1, it)
            elif not last:
                s0_ref[...] = scores(0, it + 1)
            if between is not None:
                between(hh)
            if mask is not None:
                s = jnp.where(mask, s, -jnp.inf)
            update(hh, it, s)

    def body(it, carry):
        tile(it, None)
        return carry

    s0_ref[...] = scores(0, 0)
    lax.fori_loop(0, i, body, 0)
    kv_local = lax.broadcasted_iota(jnp.int32, (tq, tq), 0)
    q_local = lax.broadcasted_iota(jnp.int32, (tq, tq), 1)
    tile(i, kv_local <= q_local, overlap, last=True)
    out = jnp.concatenate(
        [acc_ref[hh, 0:HEAD_DIM, :] / acc_ref[hh, HEAD_DIM:HEAD_DIM + 1, :] for hh in range(nh)], axis=0)
    o_ref[...] = out.T.astype(BF16)


def _block_diagonal_query(q_row):
    shape = (N_HEADS, q_row.shape[1])
    head_of_col = lax.shift_right_logical(lax.broadcasted_iota(jnp.int32, shape, 1), HEAD_DIM.bit_length() - 1)
    return jnp.where(head_of_col == lax.broadcasted_iota(jnp.int32, shape, 0), q_row, 0.0)


def _sample_init(c, q_ref, qbd_ref, m_ref, l_ref, acc_ref, fc_ref):
    @pl.when(c == 0)
    def _():
        qbd_ref[...] = _block_diagonal_query(q_ref[...]).astype(BF16)
        m_ref[...] = jnp.full_like(m_ref, -jnp.inf)
        l_ref[...] = jnp.zeros_like(l_ref)
        acc_ref[...] = jnp.zeros_like(acc_ref)
        fc_ref[...] = jnp.zeros_like(fc_ref)


def _sample_chunk_stages(k_refs, v_refs, lf_refs, qbd_ref, m_ref, l_ref, acc_ref, fc_ref, n_stages, state):
    n_pages = len(k_refs)
    per_stage = n_pages // n_stages
    assert per_stage * n_stages == n_pages
    zs = []
    run = {}

    def stage(g):
        if g >= n_stages:
            return
        if g == 0:
            row = lax.broadcasted_iota(jnp.int32, (PAGE_SIZE, PAGE_SIZE), 0)
            col = lax.broadcasted_iota(jnp.int32, (PAGE_SIZE, PAGE_SIZE), 1)
            run["upper"] = (row <= col).astype(BF16)
            run["f"] = fc_ref[...]
        qbd = qbd_ref[...]
        for r_ in range(g * per_stage, (g + 1) * per_stage):
            s = jnp.dot(qbd, k_refs[r_][...].astype(BF16), preferred_element_type=F32)
            f = run["f"]
            for piece in _split_bf16(lf_refs[r_][...] * LOG2E):
                f = f + jnp.dot(piece, run["upper"], preferred_element_type=F32)
            run["f"] = f[:, PAGE_SIZE - 1:PAGE_SIZE]
            zs.append(s - f)
        if g + 1 < n_stages:
            return
        fc_ref[...] = run["f"]
        z = jnp.concatenate(zs, axis=1)
        m_old = m_ref[...]
        m_new = jnp.maximum(m_old, jnp.max(z, axis=1, keepdims=True))
        p = jnp.exp2(z - m_new)
        alpha = jnp.exp2(m_old - m_new)
        l_new = alpha * l_ref[...] + jnp.sum(p, axis=1, keepdims=True)
        for hd in range(N_HEADS):
            rows = slice(hd * HEAD_DIM, (hd + 1) * HEAD_DIM)
            a = acc_ref[rows, :] * alpha[hd:hd + 1, :]
            for r_ in range(n_pages):
                a = a + v_refs[r_][rows, :] * p[hd:hd + 1, r_ * PAGE_SIZE:(r_ + 1) * PAGE_SIZE]
            acc_ref[rows, :] = a
        m_ref[...] = m_new
        l_ref[...] = l_new
        state.extend((run["f"], m_new, l_new))

    return stage


def _sample_finish(c, n_chunks, state, q_ref, kn_ref, vn_ref, lfn_ref, o_ref, acc_ref):
    f_run, m_new, l_new = state

    @pl.when(c == n_chunks - 1)
    def _():
        s_new = jnp.sum(_block_diagonal_query(q_ref[...]) * kn_ref[...], axis=1, keepdims=True)
        z_new = s_new - (f_run + lfn_ref[...] * LOG2E)
        m_fin = jnp.maximum(m_new, z_new)
        p_new = jnp.exp2(z_new - m_fin)
        a_fin = jnp.exp2(m_new - m_fin)
        inv_l = 1.0 / (a_fin * l_new + p_new)
        vnb = _lane_broadcast_column(vn_ref[...])
        lane0 = lax.broadcasted_iota(jnp.int32, (HEAD_DIM, LANES), 1) == 0
        for hd in range(N_HEADS):
            rows = slice(hd * HEAD_DIM, (hd + 1) * HEAD_DIM)
            a = acc_ref[rows, :] * a_fin[hd:hd + 1, :]
            a = a + jnp.where(lane0, vnb[rows, :] * p_new[hd:hd + 1, :], 0.0)
            acc_ref[rows, :] = a * inv_l[hd:hd + 1, :]
        o_ref[...] = jnp.sum(acc_ref[...].T, axis=0, keepdims=True).astype(BF16)


def _attn_kernel(pt_ref, q_ref, k_ref, vt_ref, qs_ref, kn_ref, vn_ref, lfn_ref, *refs,
                 tq, tk, nh, n_pages, n_chunks):
    del pt_ref
    k_refs = refs[0:n_pages]
    v_refs = refs[n_pages:2 * n_pages]
    lf_refs = refs[2 * n_pages:3 * n_pages]
    o_ref, os_ref, pm_ref, pacc_ref, s0_ref, qbd_ref, sm_ref, sl_ref, sacc_ref, fc_ref = refs[3 * n_pages:]
    i = pl.program_id(2)
    c = lax.rem(i, n_chunks)
    state = []
    _sample_init(c, qs_ref, qbd_ref, sm_ref, sl_ref, sacc_ref, fc_ref)
    _prompt_tile_attention(
        i, q_ref, k_ref, vt_ref, o_ref, pm_ref, pacc_ref, s0_ref, tq=tq, tk=tk, nh=nh,
        overlap=_sample_chunk_stages(k_refs, v_refs, lf_refs, qbd_ref, sm_ref, sl_ref, sacc_ref, fc_ref,
                                     SAMPLE_STAGES, state))
    _sample_finish(c, n_chunks, state, qs_ref, kn_ref, vn_ref, lfn_ref, os_ref, sacc_ref)


def _attn(qa, ka, vtb, page_table, q, kn, vn, lfn, cache_kt, cache_vt, cache_lft, layer):
    b, n_heads, t, _ = qa.shape
    n_kv, d, tk = vtb.shape[1], vtb.shape[2], vtb.shape[3]
    tq = ATTN_Q
    nh = ATTN_HEADS
    ng, nq = n_heads // nh, t // tq
    bs = q.shape[0]
    n_log = page_table.shape[1]
    n_chunks, rem = divmod(b * ng * nq, bs)
    assert rem == 0 and nq % n_chunks == 0 and n_log % n_chunks == 0
    n_pages = n_log // n_chunks
    seqs_per_row = nq // n_chunks

    def seq(bi, j, i):
        return (bi * ng + j) * seqs_per_row + i // n_chunks

    def page_map(r_):
        return lambda bi, j, i, pt: (layer, pt[seq(bi, j, i), lax.rem(i, n_chunks) * n_pages + r_], 0, 0)

    tok = lambda bi, j, i, pt: (seq(bi, j, i), 0, 0)
    in_specs = [
        pl.BlockSpec((None, nh, tq, LANES), lambda bi, j, i, pt: (bi, j, i, 0)),
        pl.BlockSpec((None, nh, t, LANES), lambda bi, j, i, pt: (bi, j, 0, 0)),
        pl.BlockSpec((None, n_kv, nh * HEAD_DIM, tk), lambda bi, j, i, pt: (bi, 0, j, 0)),
    ]
    in_specs += [pl.BlockSpec((None, 1, d), tok)] * 3 + [pl.BlockSpec((None, N_HEADS, 1), tok)]
    in_specs += [pl.BlockSpec((None, None, d, PAGE_SIZE), page_map(r_)) for r_ in range(n_pages)]
    in_specs += [pl.BlockSpec((None, None, d, PAGE_SIZE), page_map(r_)) for r_ in range(n_pages)]
    in_specs += [pl.BlockSpec((None, None, N_HEADS, PAGE_SIZE), page_map(r_)) for r_ in range(n_pages)]
    grid_spec = pltpu.PrefetchScalarGridSpec(
        num_scalar_prefetch=1,
        grid=(b, ng, nq),
        in_specs=in_specs,
        out_specs=[
            pl.BlockSpec((None, tq, nh * HEAD_DIM), lambda bi, j, i, pt: (bi, i, j)),
            pl.BlockSpec((None, 1, d), tok),
        ],
        scratch_shapes=[
            pltpu.VMEM((nh, 1, tq), F32),
            pltpu.VMEM((nh, HEAD_DIM + SUM_ROWS, tq), F32),
            pltpu.VMEM((tq, tq), F32),
            pltpu.VMEM((N_HEADS, d), BF16),
            pltpu.VMEM((N_HEADS, 1), F32),
            pltpu.VMEM((N_HEADS, 1), F32),
            pltpu.VMEM((d, LANES), F32),
            pltpu.VMEM((N_HEADS, 1), F32),
        ],
    )
    o, o_s = pl.pallas_call(
        functools.partial(_attn_kernel, tq=tq, tk=tk, nh=nh, n_pages=n_pages, n_chunks=n_chunks),
        grid_spec=grid_spec,
        out_shape=[jax.ShapeDtypeStruct((b, t, d), BF16), jax.ShapeDtypeStruct((bs, 1, d), BF16)],
        compiler_params=_params("arbitrary", "arbitrary", "arbitrary"),
        name="attn",
    )(page_table, qa, ka, vtb, q.reshape(bs, 1, d), kn.reshape(bs, 1, d), vn.reshape(bs, 1, d),
      lfn.reshape(bs, N_HEADS, 1), *([cache_kt] * n_pages), *([cache_vt] * n_pages),
      *([cache_lft] * n_pages))
    return o, o_s.reshape(bs, d)


def _feature_major_pages(cache):
    if cache.ndim == 5:
        n_layers, n_phys, page, nh, hd = cache.shape
        return jnp.transpose(cache, (0, 1, 3, 4, 2)).reshape(n_layers, n_phys, nh * hd, page)
    return jnp.transpose(cache, (0, 1, 3, 2))


def kernel(x_prompt, x_sample, cache_k, cache_v, cache_logf, state_pool, page_table, norm_mix, norm_mlp,
           norm_final, pool_w, pool_scale, w_qkv, w_f, b_f, w_o, w_up, w_down):
    b, t, d = x_prompt.shape
    bs, ts, _ = x_sample.shape
    assert ts == 1 and d == D_MODEL
    depth = norm_mix.shape[0]
    past = page_table.shape[1] * PAGE_SIZE

    ckt = _feature_major_pages(cache_k)
    cvt = _feature_major_pages(cache_v)
    clft = _feature_major_pages(cache_logf)
    ones = jnp.ones((d,), F32)

    xp = x_prompt.reshape(b * t, d)
    xs = x_sample.reshape(bs, d)
    kv_t, lp, pp = None, [], []
    ksl, vsl, lsl, psl = [], [], [], []
    for i in range(depth):
        j = i // 2
        if i % 2 == 0:
            a_p, buf = _pool_prompt(xp.reshape(b, t, d), norm_mix[i])
            a_p = a_p.reshape(b * t, d)
            pp.append(buf[:, HALO - POOL_BUF:, :])
            a_s, nst = _pool_sample(xs, jnp.transpose(state_pool[j], (1, 0, 2)), norm_mix[i], past)
            psl.append(jnp.transpose(nst, (1, 0, 2)))
            wm = pool_w[j].astype(BF16)
            sc = pool_scale[j]
        else:
            wqk = jnp.concatenate(
                [w_qkv[j, :, 0:2 * d], jnp.pad(w_f[j], ((0, 0), (0, LANES - N_HEADS)))], axis=1).astype(BF16)
            wvt = w_qkv[j, :, 2 * d:3 * d].T.astype(BF16)
            bfp = jnp.pad(b_f[j], (0, LANES - N_HEADS)).reshape(1, LANES)
            *kv_t, lft, qa, ka, vtb = _fox_proj_prompt(
                xp.reshape(b, t, d), norm_mix[i], wqk, wvt, bfp, j, depth // 2, kv_t)
            lp.append(jnp.transpose(lft, (0, 2, 1)))
            q2, k2, v2, l2 = _fox_proj_sample(xs, norm_mix[i], wqk, wvt, bfp)
            a_p, a_s = _attn(qa, ka, vtb, page_table, q2, k2, v2, l2, ckt, cvt, clft, j)
            a_p = a_p.reshape(b * t, d)
            ksl.append(k2.reshape(bs, 1, N_HEADS, HEAD_DIM))
            vsl.append(v2.reshape(bs, 1, N_HEADS, HEAD_DIM))
            lsl.append(l2.reshape(bs, 1, N_HEADS))
            wm = w_o[j].astype(BF16)
            sc = ones
        wu = w_up[i].astype(BF16)
        wd = w_down[i].astype(BF16)
        g_out = norm_final if i == depth - 1 else None
        xp = _mix_mlp(xp, a_p, wm, sc, norm_mlp[i], wu, wd, g_out)
        xs = _mix_mlp(xs, a_s, wm, sc, norm_mlp[i], wu, wd, g_out)
    y_prompt = xp.reshape(b, t, d)
    y_sample = xs.reshape(bs, 1, d)
    n_fox = kv_t[0].shape[0]
    kp, vp = [jnp.transpose(a.reshape(n_fox, b, N_HEADS, HEAD_DIM, t), (0, 1, 4, 2, 3)) for a in kv_t]
    return (y_prompt, y_sample,
            kp, vp, jnp.stack(lp), jnp.stack(pp),
            jnp.stack(ksl), jnp.stack(vsl), jnp.stack(lsl), jnp.stack(psl))
```

```python
import functools

import numpy as np
import jax
import jax.numpy as jnp
from jax import lax
from jax.experimental import pallas as pl
from jax.experimental.pallas import tpu as pltpu

F32 = jnp.float32
BF16 = jnp.bfloat16

D_MODEL = 1024
N_HEADS = 16
HEAD_DIM = 64
D_FF = 4 * D_MODEL
POOL_WINDOWS = (2, 4, 8, 16)
POOL_GROUP = D_MODEL // len(POOL_WINDOWS)
POOL_BUF = max(POOL_WINDOWS) - 1
PAGE_SIZE = 128
RMS_EPS = 1e-6

LANES = 128
HALO = 16
VMEM_LIMIT = 60 * 1024 * 1024

LOG2E = 1.4426950408889634
Q_SCALE = HEAD_DIM ** -0.5 * LOG2E

MLP_ROWS = 512
MLP_FF_CHUNK = 1024
POOL_ROWS = 512
PROJ_ROWS = 256
ATTN_Q = 512
ATTN_HEADS = 4
SAMPLE_STAGES = 2
SUM_ROWS = 16
N_SPLIT = 3


def _rmsnorm(x, g):
    r = lax.rsqrt(jnp.mean(x * x, axis=-1, keepdims=True) + RMS_EPS)
    return (x * r) * g


def _split_bf16(x):
    pieces = []
    rem = x
    for _ in range(N_SPLIT):
        p = rem.astype(BF16)
        pieces.append(p)
        rem = rem - p.astype(F32)
    return pieces


def _log_sigmoid(z):
    return jnp.minimum(z, 0.0) - jnp.log1p(jnp.exp(-jnp.abs(z)))


def _lane_broadcast_column(row):
    return jnp.broadcast_to(row, (LANES, row.shape[1])).T


def _params(*semantics):
    return pltpu.CompilerParams(dimension_semantics=semantics, vmem_limit_bytes=VMEM_LIMIT)


def _resident(shape):
    return pl.BlockSpec(shape, lambda *_: (0,) * len(shape), pipeline_mode=pl.Buffered(1))


def _pool_prompt_kernel(x_ref, halo_ref, g_ref, a_ref, buf_ref, *lvl_refs, tm, n_tiles):
    i = pl.program_id(1)
    g = g_ref[...]
    h = _rmsnorm(x_ref[...], g)
    hh = _rmsnorm(halo_ref[...], g)
    top = 2 * HALO
    end = top + tm
    e0 = lvl_refs[0]
    e0[0:HALO, :] = jnp.zeros((HALO, h.shape[1]), F32)
    e0[HALO:top, :] = jnp.where(i == 0, 0.0, hh)
    e0[top:end, :] = h
    pos = i * tm + lax.broadcasted_iota(jnp.int32, (tm, 1), 0)
    outs = []
    prev = e0
    for lvl, w in enumerate(POOL_WINDOWS):
        first = 8 * (lvl + 1)
        shift = w // 2
        drop = POOL_GROUP if lvl else 0
        if lvl + 1 < len(POOL_WINDOWS):
            cur = lvl_refs[lvl + 1]
            cur[first:end, :] = prev[first:end, drop:] + prev[first - shift:end - shift, drop:]
            s = cur[top:end, 0:POOL_GROUP]
            prev = cur
        else:
            s = prev[top:end, drop:] + prev[top - shift:end - shift, drop:]
        cnt = jnp.minimum(w, pos + 1).astype(F32)
        outs.append(s / cnt)
    pooled = jnp.concatenate(outs, axis=-1)
    a_ref[...] = (pooled - h).astype(BF16)

    @pl.when(i == n_tiles - 1)
    def _():
        buf_ref[...] = e0[end - HALO:end, :]


def _pool_prompt(x, g):
    b, t, d = x.shape
    tm = POOL_ROWS
    n_tiles = t // tm
    per = tm // HALO
    return pl.pallas_call(
        functools.partial(_pool_prompt_kernel, tm=tm, n_tiles=n_tiles),
        grid=(b, n_tiles),
        in_specs=[
            pl.BlockSpec((None, tm, d), lambda bi, i: (bi, i, 0)),
            pl.BlockSpec((None, HALO, d), lambda bi, i: (bi, jnp.maximum(i * per - 1, 0), 0)),
            pl.BlockSpec((1, d), lambda bi, i: (0, 0)),
        ],
        out_specs=[
            pl.BlockSpec((None, tm, d), lambda bi, i: (bi, i, 0)),
            pl.BlockSpec((None, HALO, d), lambda bi, i: (bi, 0, 0)),
        ],
        out_shape=[
            jax.ShapeDtypeStruct((b, t, d), BF16),
            jax.ShapeDtypeStruct((b, HALO, d), F32),
        ],
        scratch_shapes=[pltpu.VMEM((2 * HALO + tm, d - max(lvl - 1, 0) * POOL_GROUP), F32)
                        for lvl in range(len(POOL_WINDOWS))],
        compiler_params=_params("arbitrary", "arbitrary"),
        name="pool_prompt",
    )(x, x, g.reshape(1, d))


def _pool_sample_kernel(x_ref, st_ref, g_ref, a_ref, nst_ref, *, pos):
    h = _rmsnorm(x_ref[...], g_ref[...])
    outs = []
    for gi, w in enumerate(POOL_WINDOWS):
        lo, hi = gi * POOL_GROUP, (gi + 1) * POOL_GROUP
        s = h[:, lo:hi]
        for j in range(1, w):
            s = s + st_ref[POOL_BUF - j, :, lo:hi]
        outs.append(s / float(min(w, pos + 1)))
    pooled = jnp.concatenate(outs, axis=-1)
    a_ref[...] = (pooled - h).astype(BF16)
    for r in range(POOL_BUF - 1):
        nst_ref[r] = st_ref[r + 1]
    nst_ref[POOL_BUF - 1] = h


def _pool_sample(x, state_t, g, pos):
    bs, d = x.shape
    return pl.pallas_call(
        functools.partial(_pool_sample_kernel, pos=pos),
        out_shape=[
            jax.ShapeDtypeStruct((bs, d), BF16),
            jax.ShapeDtypeStruct((POOL_BUF, bs, d), F32),
        ],
        compiler_params=pltpu.CompilerParams(vmem_limit_bytes=VMEM_LIMIT),
        name="pool_sample",
    )(x, state_t, g.reshape(1, d))


def _mix_mlp_kernel(x_ref, a_ref, wm_ref, sc_ref, g_ref, wu_ref, wd_ref, *rest):
    o_ref = rest[-1]
    if len(wm_ref.shape) == 3:
        gw = wm_ref.shape[1]
        y = jnp.concatenate(
            [jnp.dot(a_ref[:, gi * gw:(gi + 1) * gw], wm_ref[gi], preferred_element_type=F32)
             for gi in range(wm_ref.shape[0])], axis=-1)
    else:
        y = jnp.dot(a_ref[...], wm_ref[...], preferred_element_type=F32)
    x1 = x_ref[...] + y * sc_ref[...]
    h = _rmsnorm(x1, g_ref[...]).astype(BF16)
    acc = x1
    for c in range(D_FF // MLP_FF_CHUNK):
        cols = slice(c * MLP_FF_CHUNK, (c + 1) * MLP_FF_CHUNK)
        u = jnp.maximum(jnp.dot(h, wu_ref[:, cols], preferred_element_type=F32), 0.0)
        acc = acc + jnp.dot((u * u).astype(BF16), wd_ref[cols, :], preferred_element_type=F32)
    o_ref[...] = _rmsnorm(acc, rest[0][...]) if len(rest) == 2 else acc


def _mix_mlp(x, a, wm, sc, g, wu_all, wd_all, layer, g_out=None):
    n, d = x.shape
    tm = min(MLP_ROWS, n)
    rows = lambda i: (i, 0)
    gains = [g.reshape(1, d)] + ([] if g_out is None else [g_out.reshape(1, d)])
    of_layer = lambda shape: pl.BlockSpec((None,) + shape, lambda i: (layer, 0, 0), pipeline_mode=pl.Buffered(1))
    return pl.pallas_call(
        _mix_mlp_kernel,
        grid=(n // tm,),
        in_specs=[
            pl.BlockSpec((tm, d), rows),
            pl.BlockSpec((tm, d), rows),
            _resident(wm.shape),
            _resident((1, d)),
            _resident((1, d)),
            of_layer((d, D_FF)),
            of_layer((D_FF, d)),
        ] + [_resident((1, d))] * (len(gains) - 1),
        out_specs=pl.BlockSpec((tm, d), rows),
        out_shape=jax.ShapeDtypeStruct((n, d), F32),
        compiler_params=_params("arbitrary"),
        name="mix_mlp",
    )(x, a, wm, sc.reshape(1, d), gains[0], wu_all, wd_all, *gains[1:])


def _aug_constants():
    width = (N_HEADS // 2) * LANES
    pq = np.zeros((LANES, width), np.float32)
    pk = np.zeros((LANES, width), np.float32)
    oq = np.zeros((1, width), np.float32)
    ok = np.zeros((1, width), np.float32)
    for h in range(N_HEADS):
        base = (h // 2) * LANES + (HEAD_DIM if h % 2 == 0 else 0)
        for p in range(N_SPLIT):
            oq[0, base + p] = 1.0
            pq[p * N_HEADS + h, base + N_SPLIT + p] = 1.0
            pk[p * N_HEADS + h, base + p] = 1.0
            ok[0, base + N_SPLIT + p] = 1.0
    return (jnp.asarray(pq, BF16), jnp.asarray(pk, BF16), jnp.asarray(oq), jnp.asarray(ok))


def _gate_logits(z, bf_ref):
    z = z + bf_ref[...]
    lane = lax.broadcasted_iota(jnp.int32, z.shape, 1)
    return jnp.where(lane < N_HEADS, _log_sigmoid(z), 0.0)


def _store_layer(ref, val):
    if len(ref.shape) == val.ndim:
        ref[...] = val
    else:
        ref[0] = val
        for other in range(1, ref.shape[0]):
            ref[other] = jnp.zeros_like(val)


def _fox_proj_prompt_kernel(x_ref, g_ref, w_ref, wvt_ref, bf_ref, pq_ref, pk_ref, oq_ref, ok_ref, *refs, tm):
    kt_ref, vt_ref, lft_ref, qa_ref, ka_ref, vtb_ref, carry_ref = refs[-7:]
    i = pl.program_id(1)

    @pl.when(i == 0)
    def _():
        carry_ref[...] = jnp.zeros_like(carry_ref)

    h = _rmsnorm(x_ref[...], g_ref[...]).astype(BF16)
    qk = jnp.dot(h, w_ref[...], preferred_element_type=F32)
    q = qk[:, 0:D_MODEL] * Q_SCALE
    k = qk[:, D_MODEL:2 * D_MODEL]
    lf = _gate_logits(qk[:, 2 * D_MODEL:2 * D_MODEL + LANES], bf_ref)
    vt = lax.dot_general(wvt_ref[...], h, (((1,), (1,)), ((), ())), preferred_element_type=F32)
    _store_layer(kt_ref, k.T)
    _store_layer(vt_ref, vt)
    vtb_ref[...] = vt.astype(BF16)
    lft_ref[...] = lf.T[0:N_HEADS, :]

    row = lax.broadcasted_iota(jnp.int32, (tm, tm), 0)
    col = lax.broadcasted_iota(jnp.int32, (tm, tm), 1)
    tri = (row >= col).astype(BF16)
    f = carry_ref[...]
    for piece in _split_bf16(lf * LOG2E):
        f = f + jnp.dot(tri, piece, preferred_element_type=F32)
    carry_ref[...] = f[tm - 1:tm, :]

    packed = None
    for p, piece in enumerate(_split_bf16(f)):
        shifted = piece.astype(F32)
        if p:
            shifted = pltpu.roll(shifted, p * N_HEADS, axis=1)
        packed = shifted if packed is None else packed + shifted
    packed = packed.astype(BF16)
    aug_q = jnp.dot(packed, pq_ref[...], preferred_element_type=F32) + oq_ref[...]
    aug_k = jnp.dot(-packed, pk_ref[...], preferred_element_type=F32) + ok_ref[...]

    lane = lax.broadcasted_iota(jnp.int32, (tm, LANES), 1)
    for hd in range(N_HEADS):
        slab = slice((hd // 2) * LANES, (hd // 2 + 1) * LANES)
        keep = (lane < HEAD_DIM) if hd % 2 == 0 else (lane >= HEAD_DIM)
        qa_ref[hd] = jnp.where(keep, q[:, slab], aug_q[:, slab]).astype(BF16)
        ka_ref[hd] = jnp.where(keep, k[:, slab], aug_k[:, slab]).astype(BF16)


def _fox_proj_prompt(x, g, wqk, wvt, bfp, layer, n_layers, prev_kv):
    b, t, d = x.shape
    tm = PROJ_ROWS
    n_tiles = t // tm
    pq, pk, oq, ok = _aug_constants()
    rows = lambda bi, i: (bi, i, 0)
    cols = lambda bi, i: (bi, 0, i)
    if prev_kv is None:
        assert layer == 0
        prev_kv, aliases = [], {}
        kv_spec = pl.BlockSpec((n_layers, None, d, tm), lambda bi, i: (0, bi, 0, i))
    else:
        aliases = {9: 0, 10: 1}
        kv_spec = pl.BlockSpec((None, None, d, tm), lambda bi, i: (layer, bi, 0, i))
    return pl.pallas_call(
        functools.partial(_fox_proj_prompt_kernel, tm=tm),
        grid=(b, n_tiles),
        in_specs=[
            pl.BlockSpec((None, tm, d), rows),
            _resident((1, d)),
            _resident(wqk.shape),
            _resident(wvt.shape),
            _resident((1, LANES)),
            _resident(pq.shape),
            _resident(pk.shape),
            _resident(oq.shape),
            _resident(ok.shape),
        ] + [pl.BlockSpec(memory_space=pl.ANY)] * len(prev_kv),
        input_output_aliases=aliases,
        out_specs=[
            kv_spec,
            kv_spec,
            pl.BlockSpec((None, N_HEADS, tm), cols),
            pl.BlockSpec((None, N_HEADS, tm, LANES), lambda bi, i: (bi, 0, i, 0)),
            pl.BlockSpec((None, N_HEADS, tm, LANES), lambda bi, i: (bi, 0, i, 0)),
            pl.BlockSpec((None, None, d, tm), lambda bi, i: (bi, i, 0, 0)),
        ],
        out_shape=[
            jax.ShapeDtypeStruct((n_layers, b, d, t), F32),
            jax.ShapeDtypeStruct((n_layers, b, d, t), F32),
            jax.ShapeDtypeStruct((b, N_HEADS, t), F32),
            jax.ShapeDtypeStruct((b, N_HEADS, t, LANES), BF16),
            jax.ShapeDtypeStruct((b, N_HEADS, t, LANES), BF16),
            jax.ShapeDtypeStruct((b, n_tiles, d, tm), BF16),
        ],
        scratch_shapes=[pltpu.VMEM((1, LANES), F32)],
        compiler_params=_params("arbitrary", "arbitrary"),
        name="fox_proj_prompt",
    )(x, g.reshape(1, d), wqk, wvt, bfp, pq, pk, oq, ok, *prev_kv)


def _fox_proj_sample_kernel(x_ref, g_ref, w_ref, wvt_ref, bf_ref, q_ref, k_ref, v_ref, lf_ref):
    h = _rmsnorm(x_ref[...], g_ref[...]).astype(BF16)
    qk = jnp.dot(h, w_ref[...], preferred_element_type=F32)
    q_ref[...] = qk[:, 0:D_MODEL] * Q_SCALE
    k_ref[...] = qk[:, D_MODEL:2 * D_MODEL]
    v_ref[...] = lax.dot_general(h, wvt_ref[...], (((1,), (1,)), ((), ())), preferred_element_type=F32)
    lf_ref[...] = _gate_logits(qk[:, 2 * D_MODEL:2 * D_MODEL + LANES], bf_ref)[:, 0:N_HEADS]


def _fox_proj_sample(x, g, wqk, wvt, bfp):
    bs, d = x.shape
    return pl.pallas_call(
        _fox_proj_sample_kernel,
        out_shape=[
            jax.ShapeDtypeStruct((bs, d), F32),
            jax.ShapeDtypeStruct((bs, d), F32),
            jax.ShapeDtypeStruct((bs, d), F32),
            jax.ShapeDtypeStruct((bs, N_HEADS), F32),
        ],
        compiler_params=pltpu.CompilerParams(vmem_limit_bytes=VMEM_LIMIT),
        name="fox_proj_sample",
    )(x, g.reshape(1, d), wqk, wvt, bfp)


def _prompt_tile_attention(i, q_ref, k_ref, vt_ref, o_ref, m_ref, acc_ref, s0_ref, *, tq, tk, nh, overlap=None):
    n_sub = tq // tk
    m_ref[...] = jnp.full_like(m_ref, -jnp.inf)
    acc_ref[...] = jnp.zeros_like(acc_ref)
    ones_rows = jnp.ones((SUM_ROWS, tk), BF16)

    def scores(hh, it):
        kb = k_ref[hh, pl.ds(pl.multiple_of(it * tq, tq), tq), :]
        return lax.dot_general(kb, q_ref[hh], (((1,), (1,)), ((), ())), preferred_element_type=F32)

    def update(hh, it, s):
        m_old = m_ref[hh]
        m_new = jnp.maximum(m_old, jnp.max(s, axis=0, keepdims=True))
        pb = jnp.exp2(s - m_new).astype(BF16)
        acc = jnp.exp2(m_old - m_new) * acc_ref[hh]
        for c in range(n_sub):
            vt = vt_ref[it * n_sub + c, hh * HEAD_DIM:(hh + 1) * HEAD_DIM, :]
            lhs = jnp.concatenate([vt, ones_rows], axis=0)
            acc = acc + jnp.dot(lhs, pb[c * tk:(c + 1) * tk, :], preferred_element_type=F32)
        acc_ref[hh] = acc
        m_ref[hh] = m_new

    def tile(it, mask, between=None, last=False):
        s_next = s0_ref[...]
        for hh in range(nh):
            s = s_next
            if hh + 1 < nh:
                s_next = scores(hh + 1, it)
            elif not last:
                s0_ref[...] = scores(0, it + 1)
            if between is not None:
                between(hh)
            if mask is not None:
                s = jnp.where(mask, s, -jnp.inf)
            update(hh, it, s)

    def body(it, carry):
        tile(it, None)
        return carry

    s0_ref[...] = scores(0, 0)
    lax.fori_loop(0, i, body, 0)
    kv_local = lax.broadcasted_iota(jnp.int32, (tq, tq), 0)
    q_local = lax.broadcasted_iota(jnp.int32, (tq, tq), 1)
    tile(i, kv_local <= q_local, overlap, last=True)
    out = jnp.concatenate(
        [acc_ref[hh, 0:HEAD_DIM, :] / acc_ref[hh, HEAD_DIM:HEAD_DIM + 1, :] for hh in range(nh)], axis=0)
    o_ref[...] = out.T.astype(BF16)


def _block_diagonal_query(q_row):
    shape = (N_HEADS, q_row.shape[1])
    head_of_col = lax.shift_right_logical(lax.broadcasted_iota(jnp.int32, shape, 1), HEAD_DIM.bit_length() - 1)
    return jnp.where(head_of_col == lax.broadcasted_iota(jnp.int32, shape, 0), q_row, 0.0)


def _sample_init(c, q_ref, qbd_ref, m_ref, l_ref, acc_ref, fc_ref):
    @pl.when(c == 0)
    def _():
        qbd_ref[...] = _block_diagonal_query(q_ref[...]).astype(BF16)
        m_ref[...] = jnp.full_like(m_ref, -jnp.inf)
        l_ref[...] = jnp.zeros_like(l_ref)
        acc_ref[...] = jnp.zeros_like(acc_ref)
        fc_ref[...] = jnp.zeros_like(fc_ref)


def _sample_chunk_stages(k_refs, v_refs, lf_refs, qbd_ref, m_ref, l_ref, acc_ref, fc_ref, n_stages, state):
    n_pages = len(k_refs)
    per_stage = n_pages // n_stages
    assert per_stage * n_stages == n_pages
    zs = []
    run = {}

    def stage(g):
        if g >= n_stages:
            return
        if g == 0:
            row = lax.broadcasted_iota(jnp.int32, (PAGE_SIZE, PAGE_SIZE), 0)
            col = lax.broadcasted_iota(jnp.int32, (PAGE_SIZE, PAGE_SIZE), 1)
            run["upper"] = (row <= col).astype(BF16)
            run["f"] = fc_ref[...]
        qbd = qbd_ref[...]
        for r_ in range(g * per_stage, (g + 1) * per_stage):
            s = jnp.dot(qbd, k_refs[r_][...].astype(BF16), preferred_element_type=F32)
            f = run["f"]
            for piece in _split_bf16(lf_refs[r_][...] * LOG2E):
                f = f + jnp.dot(piece, run["upper"], preferred_element_type=F32)
            run["f"] = f[:, PAGE_SIZE - 1:PAGE_SIZE]
            zs.append(s - f)
        if g + 1 < n_stages:
            return
        fc_ref[...] = run["f"]
        z = jnp.concatenate(zs, axis=1)
        m_old = m_ref[...]
        m_new = jnp.maximum(m_old, jnp.max(z, axis=1, keepdims=True))
        p = jnp.exp2(z - m_new)
        alpha = jnp.exp2(m_old - m_new)
        l_new = alpha * l_ref[...] + jnp.sum(p, axis=1, keepdims=True)
        for hd in range(N_HEADS):
            rows = slice(hd * HEAD_DIM, (hd + 1) * HEAD_DIM)
            a = acc_ref[rows, :] * alpha[hd:hd + 1, :]
            for r_ in range(n_pages):
                a = a + v_refs[r_][rows, :] * p[hd:hd + 1, r_ * PAGE_SIZE:(r_ + 1) * PAGE_SIZE]
            acc_ref[rows, :] = a
        m_ref[...] = m_new
        l_ref[...] = l_new
        state.extend((run["f"], m_new, l_new))

    return stage


def _sample_finish(c, n_chunks, state, q_ref, kn_ref, vn_ref, lfn_ref, o_ref, acc_ref):
    f_run, m_new, l_new = state

    @pl.when(c == n_chunks - 1)
    def _():
        s_new = jnp.sum(_block_diagonal_query(q_ref[...]) * kn_ref[...], axis=1, keepdims=True)
        z_new = s_new - (f_run + lfn_ref[...] * LOG2E)
        m_fin = jnp.maximum(m_new, z_new)
        p_new = jnp.exp2(z_new - m_fin)
        a_fin = jnp.exp2(m_new - m_fin)
        inv_l = 1.0 / (a_fin * l_new + p_new)
        vnb = _lane_broadcast_column(vn_ref[...])
        lane0 = lax.broadcasted_iota(jnp.int32, (HEAD_DIM, LANES), 1) == 0
        for hd in range(N_HEADS):
            rows = slice(hd * HEAD_DIM, (hd + 1) * HEAD_DIM)
            a = acc_ref[rows, :] * a_fin[hd:hd + 1, :]
            a = a + jnp.where(lane0, vnb[rows, :] * p_new[hd:hd + 1, :], 0.0)
            acc_ref[rows, :] = a * inv_l[hd:hd + 1, :]
        o_ref[...] = jnp.sum(acc_ref[...].T, axis=0, keepdims=True).astype(BF16)


def _attn_kernel(pt_ref, q_ref, k_ref, vt_ref, qs_ref, kn_ref, vn_ref, lfn_ref, *refs,
                 tq, tk, nh, n_pages, n_chunks):
    del pt_ref
    k_refs = refs[0:n_pages]
    v_refs = refs[n_pages:2 * n_pages]
    lf_refs = refs[2 * n_pages:3 * n_pages]
    o_ref, os_ref, pm_ref, pacc_ref, s0_ref, qbd_ref, sm_ref, sl_ref, sacc_ref, fc_ref = refs[3 * n_pages:]
    i = pl.program_id(2)
    c = lax.rem(i, n_chunks)
    state = []
    _sample_init(c, qs_ref, qbd_ref, sm_ref, sl_ref, sacc_ref, fc_ref)
    _prompt_tile_attention(
        i, q_ref, k_ref, vt_ref, o_ref, pm_ref, pacc_ref, s0_ref, tq=tq, tk=tk, nh=nh,
        overlap=_sample_chunk_stages(k_refs, v_refs, lf_refs, qbd_ref, sm_ref, sl_ref, sacc_ref, fc_ref,
                                     SAMPLE_STAGES, state))
    _sample_finish(c, n_chunks, state, qs_ref, kn_ref, vn_ref, lfn_ref, os_ref, sacc_ref)


def _attn(qa, ka, vtb, page_table, q, kn, vn, lfn, cache_kt, cache_vt, cache_lft, layer):
    b, n_heads, t, _ = qa.shape
    n_kv, d, tk = vtb.shape[1], vtb.shape[2], vtb.shape[3]
    tq = ATTN_Q
    nh = ATTN_HEADS
    ng, nq = n_heads // nh, t // tq
    bs = q.shape[0]
    n_log = page_table.shape[1]
    n_chunks, rem = divmod(b * ng * nq, bs)
    assert rem == 0 and nq % n_chunks == 0 and n_log % n_chunks == 0
    n_pages = n_log // n_chunks
    seqs_per_row = nq // n_chunks

    def seq(bi, j, i):
        return (bi * ng + j) * seqs_per_row + i // n_chunks

    def page_map(r_):
        return lambda bi, j, i, pt: (layer, pt[seq(bi, j, i), lax.rem(i, n_chunks) * n_pages + r_], 0, 0)

    tok = lambda bi, j, i, pt: (seq(bi, j, i), 0, 0)
    in_specs = [
        pl.BlockSpec((None, nh, tq, LANES), lambda bi, j, i, pt: (bi, j, i, 0)),
        pl.BlockSpec((None, nh, t, LANES), lambda bi, j, i, pt: (bi, j, 0, 0)),
        pl.BlockSpec((None, n_kv, nh * HEAD_DIM, tk), lambda bi, j, i, pt: (bi, 0, j, 0)),
    ]
    in_specs += [pl.BlockSpec((None, 1, d), tok)] * 3 + [pl.BlockSpec((None, N_HEADS, 1), tok)]
    in_specs += [pl.BlockSpec((None, None, d, PAGE_SIZE), page_map(r_)) for r_ in range(n_pages)]
    in_specs += [pl.BlockSpec((None, None, d, PAGE_SIZE), page_map(r_)) for r_ in range(n_pages)]
    in_specs += [pl.BlockSpec((None, None, N_HEADS, PAGE_SIZE), page_map(r_)) for r_ in range(n_pages)]
    grid_spec = pltpu.PrefetchScalarGridSpec(
        num_scalar_prefetch=1,
        grid=(b, ng, nq),
        in_specs=in_specs,
        out_specs=[
            pl.BlockSpec((None, tq, nh * HEAD_DIM), lambda bi, j, i, pt: (bi, i, j)),
            pl.BlockSpec((None, 1, d), tok),
        ],
        scratch_shapes=[
            pltpu.VMEM((nh, 1, tq), F32),
            pltpu.VMEM((nh, HEAD_DIM + SUM_ROWS, tq), F32),
            pltpu.VMEM((tq, tq), F32),
            pltpu.VMEM((N_HEADS, d), BF16),
            pltpu.VMEM((N_HEADS, 1), F32),
            pltpu.VMEM((N_HEADS, 1), F32),
            pltpu.VMEM((d, LANES), F32),
            pltpu.VMEM((N_HEADS, 1), F32),
        ],
    )
    o, o_s = pl.pallas_call(
        functools.partial(_attn_kernel, tq=tq, tk=tk, nh=nh, n_pages=n_pages, n_chunks=n_chunks),
        grid_spec=grid_spec,
        out_shape=[jax.ShapeDtypeStruct((b, t, d), BF16), jax.ShapeDtypeStruct((bs, 1, d), BF16)],
        compiler_params=_params("arbitrary", "arbitrary", "arbitrary"),
        name="attn",
    )(page_table, qa, ka, vtb, q.reshape(bs, 1, d), kn.reshape(bs, 1, d), vn.reshape(bs, 1, d),
      lfn.reshape(bs, N_HEADS, 1), *([cache_kt] * n_pages), *([cache_vt] * n_pages),
      *([cache_lft] * n_pages))
    return o, o_s.reshape(bs, d)


def _feature_major_pages(cache):
    if cache.ndim == 5:
        n_layers, n_phys, page, nh, hd = cache.shape
        return jnp.transpose(cache, (0, 1, 3, 4, 2)).reshape(n_layers, n_phys, nh * hd, page)
    return jnp.transpose(cache, (0, 1, 3, 2))


def kernel(x_prompt, x_sample, cache_k, cache_v, cache_logf, state_pool, page_table, norm_mix, norm_mlp,
           norm_final, pool_w, pool_scale, w_qkv, w_f, b_f, w_o, w_up, w_down):
    b, t, d = x_prompt.shape
    bs, ts, _ = x_sample.shape
    assert ts == 1 and d == D_MODEL
    depth = norm_mix.shape[0]
    past = page_table.shape[1] * PAGE_SIZE

    ckt = _feature_major_pages(cache_k)
    cvt = _feature_major_pages(cache_v)
    clft = _feature_major_pages(cache_logf)
    ones = jnp.ones((d,), F32)
    wu_all = w_up.astype(BF16)
    wd_all = w_down.astype(BF16)

    xp = x_prompt.reshape(b * t, d)
    xs = x_sample.reshape(bs, d)
    kv_t, lp, pp = None, [], []
    ksl, vsl, lsl, psl = [], [], [], []
    for i in range(depth):
        j = i // 2
        if i % 2 == 0:
            a_p, buf = _pool_prompt(xp.reshape(b, t, d), norm_mix[i])
            a_p = a_p.reshape(b * t, d)
            pp.append(buf[:, HALO - POOL_BUF:, :])
            a_s, nst = _pool_sample(xs, jnp.transpose(state_pool[j], (1, 0, 2)), norm_mix[i], past)
            psl.append(jnp.transpose(nst, (1, 0, 2)))
            wm = pool_w[j].astype(BF16)
            sc = pool_scale[j]
        else:
            wqk = jnp.concatenate(
                [w_qkv[j, :, 0:2 * d], jnp.pad(w_f[j], ((0, 0), (0, LANES - N_HEADS)))], axis=1).astype(BF16)
            wvt = w_qkv[j, :, 2 * d:3 * d].T.astype(BF16)
            bfp = jnp.pad(b_f[j], (0, LANES - N_HEADS)).reshape(1, LANES)
            *kv_t, lft, qa, ka, vtb = _fox_proj_prompt(
                xp.reshape(b, t, d), norm_mix[i], wqk, wvt, bfp, j, depth // 2, kv_t)
            lp.append(jnp.transpose(lft, (0, 2, 1)))
            q2, k2, v2, l2 = _fox_proj_sample(xs, norm_mix[i], wqk, wvt, bfp)
            a_p, a_s = _attn(qa, ka, vtb, page_table, q2, k2, v2, l2, ckt, cvt, clft, j)
            a_p = a_p.reshape(b * t, d)
            ksl.append(k2.reshape(bs, 1, N_HEADS, HEAD_DIM))
            vsl.append(v2.reshape(bs, 1, N_HEADS, HEAD_DIM))
            lsl.append(l2.reshape(bs, 1, N_HEADS))
            wm = w_o[j].astype(BF16)
            sc = ones
        g_out = norm_final if i == depth - 1 else None
        xp = _mix_mlp(xp, a_p, wm, sc, norm_mlp[i], wu_all, wd_all, i, g_out)
        xs = _mix_mlp(xs, a_s, wm, sc, norm_mlp[i], wu_all, wd_all, i, g_out)
    y_prompt = xp.reshape(b, t, d)
    y_sample = xs.reshape(bs, 1, d)
    n_fox = kv_t[0].shape[0]
    kp, vp = [jnp.transpose(a.reshape(n_fox, b, N_HEADS, HEAD_DIM, t), (0, 1, 4, 2, 3)) for a in kv_t]
    return (y_prompt, y_sample,
            kp, vp, jnp.stack(lp), jnp.stack(pp),
            jnp.stack(ksl), jnp.stack(vsl), jnp.stack(lsl), jnp.stack(psl))
```

```python
import functools

import numpy as np
import jax
import jax.numpy as jnp
from jax import lax
from jax.experimental import pallas as pl
from jax.experimental.pallas import tpu as pltpu

F32 = jnp.float32
BF16 = jnp.bfloat16

D_MODEL = 1024
N_HEADS = 16
HEAD_DIM = 64
D_FF = 4 * D_MODEL
POOL_WINDOWS = (2, 4, 8, 16)
POOL_GROUP = D_MODEL // len(POOL_WINDOWS)
POOL_BUF = max(POOL_WINDOWS) - 1
PAGE_SIZE = 128
RMS_EPS = 1e-6

LANES = 128
HALO = 16
VMEM_LIMIT = 60 * 1024 * 1024

LOG2E = 1.4426950408889634
Q_SCALE = HEAD_DIM ** -0.5 * LOG2E

MLP_ROWS = 512
MLP_FF_CHUNK = 1024
POOL_ROWS = 512
PROJ_ROWS = 256
ATTN_Q = 512
ATTN_HEADS = 4
SAMPLE_STAGES = 2
SUM_ROWS = 16
N_SPLIT = 3


def _rmsnorm(x, g):
    r = lax.rsqrt(jnp.mean(x * x, axis=-1, keepdims=True) + RMS_EPS)
    return (x * r) * g


def _split_bf16(x):
    pieces = []
    rem = x
    for _ in range(N_SPLIT):
        p = rem.astype(BF16)
        pieces.append(p)
        rem = rem - p.astype(F32)
    return pieces


def _log_sigmoid(z):
    return jnp.minimum(z, 0.0) - jnp.log1p(jnp.exp(-jnp.abs(z)))


def _lane_broadcast_column(row):
    return jnp.broadcast_to(row, (LANES, row.shape[1])).T


def _params(*semantics):
    return pltpu.CompilerParams(dimension_semantics=semantics, vmem_limit_bytes=VMEM_LIMIT)


def _resident(shape):
    return pl.BlockSpec(shape, lambda *_: (0,) * len(shape), pipeline_mode=pl.Buffered(1))


def _pool_prompt_kernel(x_ref, halo_ref, g_ref, a_ref, buf_ref, *lvl_refs, tm, n_tiles):
    i = pl.program_id(1)
    g = g_ref[...]
    h = _rmsnorm(x_ref[...], g)
    hh = _rmsnorm(halo_ref[...], g)
    top = 2 * HALO
    end = top + tm
    e0 = lvl_refs[0]
    e0[0:HALO, :] = jnp.zeros((HALO, h.shape[1]), F32)
    e0[HALO:top, :] = jnp.where(i == 0, 0.0, hh)
    e0[top:end, :] = h
    pos = i * tm + lax.broadcasted_iota(jnp.int32, (tm, 1), 0)
    outs = []
    prev = e0
    for lvl, w in enumerate(POOL_WINDOWS):
        first = 8 * (lvl + 1)
        shift = w // 2
        drop = POOL_GROUP if lvl else 0
        if lvl + 1 < len(POOL_WINDOWS):
            cur = lvl_refs[lvl + 1]
            cur[first:end, :] = prev[first:end, drop:] + prev[first - shift:end - shift, drop:]
            s = cur[top:end, 0:POOL_GROUP]
            prev = cur
        else:
            s = prev[top:end, drop:] + prev[top - shift:end - shift, drop:]
        cnt = jnp.minimum(w, pos + 1).astype(F32)
        outs.append(s / cnt)
    pooled = jnp.concatenate(outs, axis=-1)
    a_ref[...] = (pooled - h).astype(BF16)

    @pl.when(i == n_tiles - 1)
    def _():
        buf_ref[...] = e0[end - HALO:end, :]


def _pool_prompt(x, g):
    b, t, d = x.shape
    tm = POOL_ROWS
    n_tiles = t // tm
    per = tm // HALO
    return pl.pallas_call(
        functools.partial(_pool_prompt_kernel, tm=tm, n_tiles=n_tiles),
        grid=(b, n_tiles),
        in_specs=[
            pl.BlockSpec((None, tm, d), lambda bi, i: (bi, i, 0)),
            pl.BlockSpec((None, HALO, d), lambda bi, i: (bi, jnp.maximum(i * per - 1, 0), 0)),
            pl.BlockSpec((1, d), lambda bi, i: (0, 0)),
        ],
        out_specs=[
            pl.BlockSpec((None, tm, d), lambda bi, i: (bi, i, 0)),
            pl.BlockSpec((None, HALO, d), lambda bi, i: (bi, 0, 0)),
        ],
        out_shape=[
            jax.ShapeDtypeStruct((b, t, d), BF16),
            jax.ShapeDtypeStruct((b, HALO, d), F32),
        ],
        scratch_shapes=[pltpu.VMEM((2 * HALO + tm, d - max(lvl - 1, 0) * POOL_GROUP), F32)
                        for lvl in range(len(POOL_WINDOWS))],
        compiler_params=_params("arbitrary", "arbitrary"),
        name="pool_prompt",
    )(x, x, g.reshape(1, d))


def _pool_sample_kernel(x_ref, st_ref, g_ref, a_ref, nst_ref, *, pos):
    h = _rmsnorm(x_ref[...], g_ref[...])
    outs = []
    for gi, w in enumerate(POOL_WINDOWS):
        lo, hi = gi * POOL_GROUP, (gi + 1) * POOL_GROUP
        s = h[:, lo:hi]
        for j in range(1, w):
            s = s + st_ref[POOL_BUF - j, :, lo:hi]
        outs.append(s / float(min(w, pos + 1)))
    pooled = jnp.concatenate(outs, axis=-1)
    a_ref[...] = (pooled - h).astype(BF16)
    for r in range(POOL_BUF - 1):
        nst_ref[r] = st_ref[r + 1]
    nst_ref[POOL_BUF - 1] = h


def _pool_sample(x, state_t, g, pos):
    bs, d = x.shape
    return pl.pallas_call(
        functools.partial(_pool_sample_kernel, pos=pos),
        out_shape=[
            jax.ShapeDtypeStruct((bs, d), BF16),
            jax.ShapeDtypeStruct((POOL_BUF, bs, d), F32),
        ],
        compiler_params=pltpu.CompilerParams(vmem_limit_bytes=VMEM_LIMIT),
        name="pool_sample",
    )(x, state_t, g.reshape(1, d))


def _mix_mlp_kernel(x_ref, a_ref, wm_ref, sc_ref, g_ref, wu_ref, wd_ref, *rest):
    o_ref = rest[-1]
    if len(wm_ref.shape) == 3:
        gw = wm_ref.shape[1]
        y = jnp.concatenate(
            [jnp.dot(a_ref[:, gi * gw:(gi + 1) * gw], wm_ref[gi], preferred_element_type=F32)
             for gi in range(wm_ref.shape[0])], axis=-1)
    else:
        y = jnp.dot(a_ref[...], wm_ref[...], preferred_element_type=F32)
    x1 = x_ref[...] + y * sc_ref[...]
    h = _rmsnorm(x1, g_ref[...]).astype(BF16)
    acc = x1
    for c in range(D_FF // MLP_FF_CHUNK):
        cols = slice(c * MLP_FF_CHUNK, (c + 1) * MLP_FF_CHUNK)
        u = jnp.maximum(jnp.dot(h, wu_ref[:, cols], preferred_element_type=F32), 0.0)
        acc = acc + jnp.dot((u * u).astype(BF16), wd_ref[cols, :], preferred_element_type=F32)
    o_ref[...] = _rmsnorm(acc, rest[0][...]) if len(rest) == 2 else acc


def _mix_mlp(x, a, wm, sc, g, wu_all, wd_all, layer, g_out=None):
    n, d = x.shape
    tm = min(MLP_ROWS, n)
    rows = lambda i: (i, 0)
    gains = [g.reshape(1, d)] + ([] if g_out is None else [g_out.reshape(1, d)])
    of_layer = lambda shape: pl.BlockSpec((None,) + shape, lambda i: (layer, 0, 0), pipeline_mode=pl.Buffered(1))
    return pl.pallas_call(
        _mix_mlp_kernel,
        grid=(n // tm,),
        in_specs=[
            pl.BlockSpec((tm, d), rows),
            pl.BlockSpec((tm, d), rows),
            _resident(wm.shape),
            _resident((1, d)),
            _resident((1, d)),
            of_layer((d, D_FF)),
            of_layer((D_FF, d)),
        ] + [_resident((1, d))] * (len(gains) - 1),
        out_specs=pl.BlockSpec((tm, d), rows),
        out_shape=jax.ShapeDtypeStruct((n, d), F32),
        compiler_params=_params("arbitrary"),
        name="mix_mlp",
    )(x, a, wm, sc.reshape(1, d), gains[0], wu_all, wd_all, *gains[1:])


def _aug_constants():
    width = (N_HEADS // 2) * LANES
    pq = np.zeros((LANES, width), np.float32)
    pk = np.zeros((LANES, width), np.float32)
    oq = np.zeros((1, width), np.float32)
    ok = np.zeros((1, width), np.float32)
    for h in range(N_HEADS):
        base = (h // 2) * LANES + (HEAD_DIM if h % 2 == 0 else 0)
        for p in range(N_SPLIT):
            oq[0, base + p] = 1.0
            pq[p * N_HEADS + h, base + N_SPLIT + p] = 1.0
            pk[p * N_HEADS + h, base + p] = 1.0
            ok[0, base + N_SPLIT + p] = 1.0
    return (jnp.asarray(pq, BF16), jnp.asarray(pk, BF16), jnp.asarray(oq), jnp.asarray(ok))


def _gate_logits(z, bf_ref):
    z = z + bf_ref[...]
    lane = lax.broadcasted_iota(jnp.int32, z.shape, 1)
    return jnp.where(lane < N_HEADS, _log_sigmoid(z), 0.0)


def _store_layer(ref, val):
    if len(ref.shape) == val.ndim:
        ref[...] = val
    else:
        ref[0] = val
        for other in range(1, ref.shape[0]):
            ref[other] = jnp.zeros_like(val)


def _fox_proj_prompt_kernel(x_ref, g_ref, w_ref, wvt_ref, bf_ref, pq_ref, pk_ref, oq_ref, ok_ref, *refs, tm):
    kt_ref, vt_ref, lft_ref, qa_ref, ka_ref, vtb_ref, carry_ref = refs[-7:]
    i = pl.program_id(1)

    @pl.when(i == 0)
    def _():
        carry_ref[...] = jnp.zeros_like(carry_ref)

    h = _rmsnorm(x_ref[...], g_ref[...]).astype(BF16)
    qk = jnp.dot(h, w_ref[...], preferred_element_type=F32)
    q = qk[:, 0:D_MODEL] * Q_SCALE
    k = qk[:, D_MODEL:2 * D_MODEL]
    lf = _gate_logits(qk[:, 2 * D_MODEL:2 * D_MODEL + LANES], bf_ref)
    vt = lax.dot_general(wvt_ref[...], h, (((1,), (1,)), ((), ())), preferred_element_type=F32)
    _store_layer(kt_ref, k.T)
    _store_layer(vt_ref, vt)
    vtb_ref[...] = vt.astype(BF16)
    lft_ref[...] = lf.T[0:N_HEADS, :]

    row = lax.broadcasted_iota(jnp.int32, (tm, tm), 0)
    col = lax.broadcasted_iota(jnp.int32, (tm, tm), 1)
    tri = (row >= col).astype(BF16)
    f = carry_ref[...]
    for piece in _split_bf16(lf * LOG2E):
        f = f + jnp.dot(tri, piece, preferred_element_type=F32)
    carry_ref[...] = f[tm - 1:tm, :]

    packed = None
    for p, piece in enumerate(_split_bf16(f)):
        shifted = piece.astype(F32)
        if p:
            shifted = pltpu.roll(shifted, p * N_HEADS, axis=1)
        packed = shifted if packed is None else packed + shifted
    packed = packed.astype(BF16)
    aug_q = jnp.dot(packed, pq_ref[...], preferred_element_type=F32) + oq_ref[...]
    aug_k = jnp.dot(-packed, pk_ref[...], preferred_element_type=F32) + ok_ref[...]

    lane = lax.broadcasted_iota(jnp.int32, (tm, LANES), 1)
    for hd in range(N_HEADS):
        slab = slice((hd // 2) * LANES, (hd // 2 + 1) * LANES)
        keep = (lane < HEAD_DIM) if hd % 2 == 0 else (lane >= HEAD_DIM)
        qa_ref[hd] = jnp.where(keep, q[:, slab], aug_q[:, slab]).astype(BF16)
        ka_ref[hd] = jnp.where(keep, k[:, slab], aug_k[:, slab]).astype(BF16)


def _fox_proj_prompt(x, g, wqk, wvt, bfp, layer, n_layers, prev_kv):
    b, t, d = x.shape
    tm = PROJ_ROWS
    n_tiles = t // tm
    pq, pk, oq, ok = _aug_constants()
    rows = lambda bi, i: (bi, i, 0)
    cols = lambda bi, i: (bi, 0, i)
    if prev_kv is None:
        assert layer == 0
        prev_kv, aliases = [], {}
        kv_spec = pl.BlockSpec((n_layers, None, d, tm), lambda bi, i: (0, bi, 0, i))
    else:
        aliases = {9: 0, 10: 1}
        kv_spec = pl.BlockSpec((None, None, d, tm), lambda bi, i: (layer, bi, 0, i))
    return pl.pallas_call(
        functools.partial(_fox_proj_prompt_kernel, tm=tm),
        grid=(b, n_tiles),
        in_specs=[
            pl.BlockSpec((None, tm, d), rows),
            _resident((1, d)),
            _resident(wqk.shape),
            _resident(wvt.shape),
            _resident((1, LANES)),
            _resident(pq.shape),
            _resident(pk.shape),
            _resident(oq.shape),
            _resident(ok.shape),
        ] + [pl.BlockSpec(memory_space=pl.ANY)] * len(prev_kv),
        input_output_aliases=aliases,
        out_specs=[
            kv_spec,
            kv_spec,
            pl.BlockSpec((None, N_HEADS, tm), cols),
            pl.BlockSpec((None, N_HEADS, tm, LANES), lambda bi, i: (bi, 0, i, 0)),
            pl.BlockSpec((None, N_HEADS, tm, LANES), lambda bi, i: (bi, 0, i, 0)),
            pl.BlockSpec((None, None, d, tm), lambda bi, i: (bi, i, 0, 0)),
        ],
        out_shape=[
            jax.ShapeDtypeStruct((n_layers, b, d, t), F32),
            jax.ShapeDtypeStruct((n_layers, b, d, t), F32),
            jax.ShapeDtypeStruct((b, N_HEADS, t), F32),
            jax.ShapeDtypeStruct((b, N_HEADS, t, LANES), BF16),
            jax.ShapeDtypeStruct((b, N_HEADS, t, LANES), BF16),
            jax.ShapeDtypeStruct((b, n_tiles, d, tm), BF16),
        ],
        scratch_shapes=[pltpu.VMEM((1, LANES), F32)],
        compiler_params=_params("arbitrary", "arbitrary"),
        name="fox_proj_prompt",
    )(x, g.reshape(1, d), wqk, wvt, bfp, pq, pk, oq, ok, *prev_kv)


def _fox_proj_sample_kernel(x_ref, g_ref, w_ref, wvt_ref, bf_ref, q_ref, k_ref, v_ref, lf_ref):
    h = _rmsnorm(x_ref[...], g_ref[...]).astype(BF16)
    qk = jnp.dot(h, w_ref[...], preferred_element_type=F32)
    q_ref[...] = qk[:, 0:D_MODEL] * Q_SCALE
    k_ref[...] = qk[:, D_MODEL:2 * D_MODEL]
    v_ref[...] = lax.dot_general(h, wvt_ref[...], (((1,), (1,)), ((), ())), preferred_element_type=F32)
    lf_ref[...] = _gate_logits(qk[:, 2 * D_MODEL:2 * D_MODEL + LANES], bf_ref)[:, 0:N_HEADS]


def _fox_proj_sample(x, g, wqk, wvt, bfp):
    bs, d = x.shape
    return pl.pallas_call(
        _fox_proj_sample_kernel,
        out_shape=[
            jax.ShapeDtypeStruct((bs, d), F32),
            jax.ShapeDtypeStruct((bs, d), F32),
            jax.ShapeDtypeStruct((bs, d), F32),
            jax.ShapeDtypeStruct((bs, N_HEADS), F32),
        ],
        compiler_params=pltpu.CompilerParams(vmem_limit_bytes=VMEM_LIMIT),
        name="fox_proj_sample",
    )(x, g.reshape(1, d), wqk, wvt, bfp)


def _prompt_tile_attention(i, q_ref, k_ref, vt_ref, o_ref, m_ref, acc_ref, s0_ref, *, tq, tk, nh, overlap=None):
    n_sub = tq // tk
    m_ref[...] = jnp.full_like(m_ref, -jnp.inf)
    acc_ref[...] = jnp.zeros_like(acc_ref)
    ones_rows = jnp.ones((SUM_ROWS, tk), BF16)

    def scores(hh, it):
        kb = k_ref[hh, pl.ds(pl.multiple_of(it * tq, tq), tq), :]
        return lax.dot_general(kb, q_ref[hh], (((1,), (1,)), ((), ())), preferred_element_type=F32)

    def update(hh, it, s, diag=False):
        m_old = m_ref[hh]
        m_new = jnp.maximum(m_old, jnp.max(s, axis=0, keepdims=True))
        acc = jnp.exp2(m_old - m_new) * acc_ref[hh]
        blocks = [acc[:, j * tk:(j + 1) * tk] for j in range(n_sub)]
        for c in range(n_sub):
            first = c if diag else 0
            pc = jnp.exp2(s[c * tk:(c + 1) * tk, first * tk:] - m_new[:, first * tk:]).astype(BF16)
            vt = vt_ref[it * n_sub + c, hh * HEAD_DIM:(hh + 1) * HEAD_DIM, :]
            lhs = jnp.concatenate([vt, ones_rows], axis=0)
            part = jnp.dot(lhs, pc, preferred_element_type=F32)
            for j in range(first, n_sub):
                blocks[j] = blocks[j] + part[:, (j - first) * tk:(j - first + 1) * tk]
        acc_ref[hh] = jnp.concatenate(blocks, axis=1)
        m_ref[hh] = m_new

    def tile(it, mask, between=None, last=False):
        s_next = s0_ref[...]
        for hh in range(nh):
            s = s_next
            if hh + 1 < nh:
                s_next = scores(hh + 1, it)
            elif not last:
                s0_ref[...] = scores(0, it + 1)
            if between is not None:
                between(hh)
            if mask is not None:
                s = jnp.where(mask, s, -jnp.inf)
            update(hh, it, s, diag=last)

    def body(it, carry):
        tile(it, None)
        return carry

    s0_ref[...] = scores(0, 0)
    lax.fori_loop(0, i, body, 0)
    kv_local = lax.broadcasted_iota(jnp.int32, (tq, tq), 0)
    q_local = lax.broadcasted_iota(jnp.int32, (tq, tq), 1)
    tile(i, kv_local <= q_local, overlap, last=True)
    out = jnp.concatenate(
        [acc_ref[hh, 0:HEAD_DIM, :] / acc_ref[hh, HEAD_DIM:HEAD_DIM + 1, :] for hh in range(nh)], axis=0)
    o_ref[...] = out.T.astype(BF16)


def _block_diagonal_query(q_row):
    shape = (N_HEADS, q_row.shape[1])
    head_of_col = lax.shift_right_logical(lax.broadcasted_iota(jnp.int32, shape, 1), HEAD_DIM.bit_length() - 1)
    return jnp.where(head_of_col == lax.broadcasted_iota(jnp.int32, shape, 0), q_row, 0.0)


def _sample_init(c, q_ref, qbd_ref, m_ref, l_ref, acc_ref, fc_ref):
    @pl.when(c == 0)
    def _():
        qbd_ref[...] = _block_diagonal_query(q_ref[...]).astype(BF16)
        m_ref[...] = jnp.full_like(m_ref, -jnp.inf)
        l_ref[...] = jnp.zeros_like(l_ref)
        acc_ref[...] = jnp.zeros_like(acc_ref)
        fc_ref[...] = jnp.zeros_like(fc_ref)


def _sample_chunk_stages(k_refs, v_refs, lf_refs, qbd_ref, m_ref, l_ref, acc_ref, fc_ref, n_stages, state):
    n_pages = len(k_refs)
    per_stage = n_pages // n_stages
    assert per_stage * n_stages == n_pages
    zs = []
    run = {}

    def stage(g):
        if g >= n_stages:
            return
        if g == 0:
            row = lax.broadcasted_iota(jnp.int32, (PAGE_SIZE, PAGE_SIZE), 0)
            col = lax.broadcasted_iota(jnp.int32, (PAGE_SIZE, PAGE_SIZE), 1)
            run["upper"] = (row <= col).astype(BF16)
            run["f"] = fc_ref[...]
        qbd = qbd_ref[...]
        for r_ in range(g * per_stage, (g + 1) * per_stage):
            s = jnp.dot(qbd, k_refs[r_][...].astype(BF16), preferred_element_type=F32)
            f = run["f"]
            for piece in _split_bf16(lf_refs[r_][...] * LOG2E):
                f = f + jnp.dot(piece, run["upper"], preferred_element_type=F32)
            run["f"] = f[:, PAGE_SIZE - 1:PAGE_SIZE]
            zs.append(s - f)
        if g + 1 < n_stages:
            return
        fc_ref[...] = run["f"]
        z = jnp.concatenate(zs, axis=1)
        m_old = m_ref[...]
        m_new = jnp.maximum(m_old, jnp.max(z, axis=1, keepdims=True))
        p = jnp.exp2(z - m_new)
        alpha = jnp.exp2(m_old - m_new)
        l_new = alpha * l_ref[...] + jnp.sum(p, axis=1, keepdims=True)
        for hd in range(N_HEADS):
            rows = slice(hd * HEAD_DIM, (hd + 1) * HEAD_DIM)
            a = acc_ref[rows, :] * alpha[hd:hd + 1, :]
            for r_ in range(n_pages):
                a = a + v_refs[r_][rows, :] * p[hd:hd + 1, r_ * PAGE_SIZE:(r_ + 1) * PAGE_SIZE]
            acc_ref[rows, :] = a
        m_ref[...] = m_new
        l_ref[...] = l_new
        state.extend((run["f"], m_new, l_new))

    return stage


def _sample_finish(c, n_chunks, state, q_ref, kn_ref, vn_ref, lfn_ref, o_ref, acc_ref):
    f_run, m_new, l_new = state

    @pl.when(c == n_chunks - 1)
    def _():
        s_new = jnp.sum(_block_diagonal_query(q_ref[...]) * kn_ref[...], axis=1, keepdims=True)
        z_new = s_new - (f_run + lfn_ref[...] * LOG2E)
        m_fin = jnp.maximum(m_new, z_new)
        p_new = jnp.exp2(z_new - m_fin)
        a_fin = jnp.exp2(m_new - m_fin)
        inv_l = 1.0 / (a_fin * l_new + p_new)
        vnb = _lane_broadcast_column(vn_ref[...])
        lane0 = lax.broadcasted_iota(jnp.int32, (HEAD_DIM, LANES), 1) == 0
        for hd in range(N_HEADS):
            rows = slice(hd * HEAD_DIM, (hd + 1) * HEAD_DIM)
            a = acc_ref[rows, :] * a_fin[hd:hd + 1, :]
            a = a + jnp.where(lane0, vnb[rows, :] * p_new[hd:hd + 1, :], 0.0)
            acc_ref[rows, :] = a * inv_l[hd:hd + 1, :]
        o_ref[...] = jnp.sum(acc_ref[...].T, axis=0, keepdims=True).astype(BF16)


def _attn_kernel(pt_ref, q_ref, k_ref, vt_ref, qs_ref, kn_ref, vn_ref, lfn_ref, *refs,
                 tq, tk, nh, n_pages, n_chunks):
    del pt_ref
    k_refs = refs[0:n_pages]
    v_refs = refs[n_pages:2 * n_pages]
    lf_refs = refs[2 * n_pages:3 * n_pages]
    o_ref, os_ref, pm_ref, pacc_ref, s0_ref, qbd_ref, sm_ref, sl_ref, sacc_ref, fc_ref = refs[3 * n_pages:]
    i = pl.program_id(2)
    c = lax.rem(i, n_chunks)
    state = []
    _sample_init(c, qs_ref, qbd_ref, sm_ref, sl_ref, sacc_ref, fc_ref)
    _prompt_tile_attention(
        i, q_ref, k_ref, vt_ref, o_ref, pm_ref, pacc_ref, s0_ref, tq=tq, tk=tk, nh=nh,
        overlap=_sample_chunk_stages(k_refs, v_refs, lf_refs, qbd_ref, sm_ref, sl_ref, sacc_ref, fc_ref,
                                     SAMPLE_STAGES, state))
    _sample_finish(c, n_chunks, state, qs_ref, kn_ref, vn_ref, lfn_ref, os_ref, sacc_ref)


def _attn(qa, ka, vtb, page_table, q, kn, vn, lfn, cache_kt, cache_vt, cache_lft, layer):
    b, n_heads, t, _ = qa.shape
    n_kv, d, tk = vtb.shape[1], vtb.shape[2], vtb.shape[3]
    tq = ATTN_Q
    nh = ATTN_HEADS
    ng, nq = n_heads // nh, t // tq
    bs = q.shape[0]
    n_log = page_table.shape[1]
    n_chunks, rem = divmod(b * ng * nq, bs)
    assert rem == 0 and nq % n_chunks == 0 and n_log % n_chunks == 0
    n_pages = n_log // n_chunks
    seqs_per_row = nq // n_chunks

    def seq(bi, j, i):
        return (bi * ng + j) * seqs_per_row + i // n_chunks

    def page_map(r_):
        return lambda bi, j, i, pt: (layer, pt[seq(bi, j, i), lax.rem(i, n_chunks) * n_pages + r_], 0, 0)

    tok = lambda bi, j, i, pt: (seq(bi, j, i), 0, 0)
    in_specs = [
        pl.BlockSpec((None, nh, tq, LANES), lambda bi, j, i, pt: (bi, j, i, 0)),
        pl.BlockSpec((None, nh, t, LANES), lambda bi, j, i, pt: (bi, j, 0, 0)),
        pl.BlockSpec((None, n_kv, nh * HEAD_DIM, tk), lambda bi, j, i, pt: (bi, 0, j, 0)),
    ]
    in_specs += [pl.BlockSpec((None, 1, d), tok)] * 3 + [pl.BlockSpec((None, N_HEADS, 1), tok)]
    in_specs += [pl.BlockSpec((None, None, d, PAGE_SIZE), page_map(r_)) for r_ in range(n_pages)]
    in_specs += [pl.BlockSpec((None, None, d, PAGE_SIZE), page_map(r_)) for r_ in range(n_pages)]
    in_specs += [pl.BlockSpec((None, None, N_HEADS, PAGE_SIZE), page_map(r_)) for r_ in range(n_pages)]
    grid_spec = pltpu.PrefetchScalarGridSpec(
        num_scalar_prefetch=1,
        grid=(b, ng, nq),
        in_specs=in_specs,
        out_specs=[
            pl.BlockSpec((None, tq, nh * HEAD_DIM), lambda bi, j, i, pt: (bi, i, j)),
            pl.BlockSpec((None, 1, d), tok),
        ],
        scratch_shapes=[
            pltpu.VMEM((nh, 1, tq), F32),
            pltpu.VMEM((nh, HEAD_DIM + SUM_ROWS, tq), F32),
            pltpu.VMEM((tq, tq), F32),
            pltpu.VMEM((N_HEADS, d), BF16),
            pltpu.VMEM((N_HEADS, 1), F32),
            pltpu.VMEM((N_HEADS, 1), F32),
            pltpu.VMEM((d, LANES), F32),
            pltpu.VMEM((N_HEADS, 1), F32),
        ],
    )
    o, o_s = pl.pallas_call(
        functools.partial(_attn_kernel, tq=tq, tk=tk, nh=nh, n_pages=n_pages, n_chunks=n_chunks),
        grid_spec=grid_spec,
        out_shape=[jax.ShapeDtypeStruct((b, t, d), BF16), jax.ShapeDtypeStruct((bs, 1, d), BF16)],
        compiler_params=_params("arbitrary", "arbitrary", "arbitrary"),
        name="attn",
    )(page_table, qa, ka, vtb, q.reshape(bs, 1, d), kn.reshape(bs, 1, d), vn.reshape(bs, 1, d),
      lfn.reshape(bs, N_HEADS, 1), *([cache_kt] * n_pages), *([cache_vt] * n_pages),
      *([cache_lft] * n_pages))
    return o, o_s.reshape(bs, d)


def _feature_major_pages(cache):
    if cache.ndim == 5:
        n_layers, n_phys, page, nh, hd = cache.shape
        return jnp.transpose(cache, (0, 1, 3, 4, 2)).reshape(n_layers, n_phys, nh * hd, page)
    return jnp.transpose(cache, (0, 1, 3, 2))


def kernel(x_prompt, x_sample, cache_k, cache_v, cache_logf, state_pool, page_table, norm_mix, norm_mlp,
           norm_final, pool_w, pool_scale, w_qkv, w_f, b_f, w_o, w_up, w_down):
    b, t, d = x_prompt.shape
    bs, ts, _ = x_sample.shape
    assert ts == 1 and d == D_MODEL
    depth = norm_mix.shape[0]
    past = page_table.shape[1] * PAGE_SIZE

    ckt = _feature_major_pages(cache_k)
    cvt = _feature_major_pages(cache_v)
    clft = _feature_major_pages(cache_logf)
    ones = jnp.ones((d,), F32)
    wu_all = w_up.astype(BF16)
    wd_all = w_down.astype(BF16)

    xp = x_prompt.reshape(b * t, d)
    xs = x_sample.reshape(bs, d)
    kv_t, lp, pp = None, [], []
    ksl, vsl, lsl, psl = [], [], [], []
    for i in range(depth):
        j = i // 2
        if i % 2 == 0:
            a_p, buf = _pool_prompt(xp.reshape(b, t, d), norm_mix[i])
            a_p = a_p.reshape(b * t, d)
            pp.append(buf[:, HALO - POOL_BUF:, :])
            a_s, nst = _pool_sample(xs, jnp.transpose(state_pool[j], (1, 0, 2)), norm_mix[i], past)
            psl.append(jnp.transpose(nst, (1, 0, 2)))
            wm = pool_w[j].astype(BF16)
            sc = pool_scale[j]
        else:
            wqk = jnp.concatenate(
                [w_qkv[j, :, 0:2 * d], jnp.pad(w_f[j], ((0, 0), (0, LANES - N_HEADS)))], axis=1).astype(BF16)
            wvt = w_qkv[j, :, 2 * d:3 * d].T.astype(BF16)
            bfp = jnp.pad(b_f[j], (0, LANES - N_HEADS)).reshape(1, LANES)
            *kv_t, lft, qa, ka, vtb = _fox_proj_prompt(
                xp.reshape(b, t, d), norm_mix[i], wqk, wvt, bfp, j, depth // 2, kv_t)
            lp.append(jnp.transpose(lft, (0, 2, 1)))
            q2, k2, v2, l2 = _fox_proj_sample(xs, norm_mix[i], wqk, wvt, bfp)
            a_p, a_s = _attn(qa, ka, vtb, page_table, q2, k2, v2, l2, ckt, cvt, clft, j)
            a_p = a_p.reshape(b * t, d)
            ksl.append(k2.reshape(bs, 1, N_HEADS, HEAD_DIM))
            vsl.append(v2.reshape(bs, 1, N_HEADS, HEAD_DIM))
            lsl.append(l2.reshape(bs, 1, N_HEADS))
            wm = w_o[j].astype(BF16)
            sc = ones
        g_out = norm_final if i == depth - 1 else None
        xp = _mix_mlp(xp, a_p, wm, sc, norm_mlp[i], wu_all, wd_all, i, g_out)
        xs = _mix_mlp(xs, a_s, wm, sc, norm_mlp[i], wu_all, wd_all, i, g_out)
    y_prompt = xp.reshape(b, t, d)
    y_sample = xs.reshape(bs, 1, d)
    n_fox = kv_t[0].shape[0]
    kp, vp = [jnp.transpose(a.reshape(n_fox, b, N_HEADS, HEAD_DIM, t), (0, 1, 4, 2, 3)) for a in kv_t]
    return (y_prompt, y_sample,
            kp, vp, jnp.stack(lp), jnp.stack(pp),
            jnp.stack(ksl), jnp.stack(vsl), jnp.stack(lsl), jnp.stack(psl))
```
